```python
import math
import jax, jax.numpy as jnp
from jax import lax
import numpy as np

D_MODEL = 1024
BATCH = 4
SEQ = 8192
DEPTH = 1

D_MIX = D_MODEL
RET_HEADS = 4
RET_HEAD_DIM = (D_MIX // 2) // RET_HEADS
RET_WIDTH = RET_HEADS * RET_HEAD_DIM
RET_CHUNK = 128
ATT_HEADS = 8
ATT_HEAD_DIM = (D_MIX - RET_WIDTH) // ATT_HEADS
ATT_WIDTH = ATT_HEADS * ATT_HEAD_DIM
DILATED_PATTERN = ((128, 1), (512, 4), (2048, 16))
BAND_BLOCK = 128
D_FF = 2816
ROPE_BASE = 10000.0
NORM_EPS = 1e-6
GN_EPS = 1e-6
IN_COLS = 4 * RET_WIDTH + 3 * ATT_WIDTH

kernel_name = "hybrid_retention_dilated_macaron"


def rms_norm(x, g):
    xf = x.astype(jnp.float32)
    y = xf * lax.rsqrt(jnp.mean(xf * xf, axis=-1, keepdims=True) + NORM_EPS)
    return (y * g.astype(jnp.float32)).astype(x.dtype)


def swiglu(x, w_gate, w_up, w_down):
    return (jax.nn.silu(x @ w_gate) * (x @ w_up)) @ w_down


def split_heads(t, n_heads):
    b, s, _ = t.shape
    return t.reshape(b, s, n_heads, -1).transpose(0, 2, 1, 3)


def rotate_every_two(t):
    t1 = t[..., ::2]
    t2 = t[..., 1::2]
    return jnp.stack((-t2, t1), axis=-1).reshape(t.shape)


def apply_rotary(t):
    s, d = t.shape[-2], t.shape[-1]
    pos = jnp.arange(s, dtype=jnp.float32)
    inv_freq = ROPE_BASE ** (-jnp.arange(0, d, 2, dtype=jnp.float32) / d)
    ang = jnp.repeat(pos[:, None] * inv_freq[None, :], 2, axis=-1)
    cos = jnp.cos(ang).astype(t.dtype)
    sin = jnp.sin(ang).astype(t.dtype)
    return t * cos + rotate_every_two(t) * sin


def chunkwise_retention(q, k, v):
    b, h, s, dk = q.shape
    dv = v.shape[-1]
    c = RET_CHUNK
    n = s // c
    dt = q.dtype
    log_g = jnp.log(1.0 - 2.0 ** (-5.0 - jnp.arange(h, dtype=jnp.float32)))
    idx = jnp.arange(c, dtype=jnp.float32)
    rel = idx[:, None] - idx[None, :]
    decay_in = jnp.where(rel >= 0, jnp.exp(log_g[:, None, None] * jnp.maximum(rel, 0.0)), 0.0)
    zeta = jnp.exp(log_g[:, None] * (c - 1 - idx)[None, :])
    xi = jnp.exp(log_g[:, None] * (idx + 1)[None, :])
    chunk_decay = jnp.exp(log_g * c)
    qc = q.reshape(b, h, n, c, dk)
    kc = k.reshape(b, h, n, c, dk)
    vc = v.reshape(b, h, n, c, dv)
    scores = jnp.einsum('bhncd,bhnmd->bhncm', qc, kc) * decay_in[None, :, None].astype(dt)
    intra = jnp.einsum('bhncm,bhnme->bhnce', scores, vc)
    kv = jnp.einsum('bhncd,bhnce->nbhde', kc * zeta[None, :, None, :, None].astype(dt), vc)
    gamma_c = chunk_decay[None, :, None, None].astype(dt)

    def step(state, kv_n):
        return state * gamma_c + kv_n, state

    _, prev_states = lax.scan(step, jnp.zeros((b, h, dk, dv), dt), kv)
    inter = jnp.einsum('bhncd,nbhde->bhnce', qc, prev_states) * xi[None, :, None, :, None].astype(dt)
    return (intra + inter).reshape(b, h, s, dv)


def dilated_window_branch(q, k, v, window, dilation):
    b, h, s, hd = q.shape
    L = s // dilation
    span = window // dilation
    blk = BAND_BLOCK
    nb = -(-L // blk)
    lp = nb * blk

    def to_blocks(t):
        t = t.reshape(b, h, L, dilation, hd).transpose(0, 1, 3, 2, 4)
        t = jnp.pad(t, ((0, 0), (0, 0), (0, 0), (0, lp - L), (0, 0)))
        return t.reshape(b, h, dilation, nb, blk, hd)

    def with_prev(t):
        prev = jnp.pad(t[:, :, :, :-1], ((0, 0), (0, 0), (0, 0), (1, 0), (0, 0), (0, 0)))
        return jnp.concatenate([prev, t], axis=4)

    qb = to_blocks(q)
    kb = with_prev(to_blocks(k))
    vb = with_prev(to_blocks(v))
    sc = jnp.einsum('bhgnqd,bhgnkd->bhgnqk', qb, kb).astype(jnp.float32)
    qi = jnp.arange(blk)[:, None]
    kj = jnp.arange(2 * blk)[None, :]
    dist = qi + blk - kj
    bidx = jnp.arange(nb)[:, None, None]
    mask = (dist >= 0) & (dist <= span) & (bidx * blk + kj - blk >= 0)
    sc = jnp.where(mask, sc, -jnp.inf)
    m = jnp.max(sc, axis=-1, keepdims=True)
    e = jnp.exp(sc - m)
    denom = jnp.sum(e, axis=-1, keepdims=True)
    lse = (m + jnp.log(denom))[..., 0]
    o = jnp.einsum('bhgnqk,bhgnkd->bhgnqd', (e / denom).astype(v.dtype), vb)
    o = o.reshape(b, h, dilation, lp, hd)[:, :, :, :L].transpose(0, 1, 3, 2, 4).reshape(b, h, s, hd)
    lse = lse.reshape(b, h, dilation, lp)[..., :L].transpose(0, 1, 3, 2).reshape(b, h, s)
    return o, lse


def dilated_attention(q, k, v):
    outs, lses = [], []
    for window, dilation in DILATED_PATTERN:
        o, l = dilated_window_branch(q, k, v, window, dilation)
        outs.append(o)
        lses.append(l)
    wts = jax.nn.softmax(jnp.stack(lses, axis=0), axis=0)
    return jnp.einsum('pbhs,pbhsd->bhsd', wts.astype(q.dtype), jnp.stack(outs, axis=0))


def setup_inputs(seed: int = 0) -> dict:
    key = jax.random.key(seed)
    ks = jax.random.split(key, 16)
    f32 = jnp.float32

    def w(k_, shape, fan_in):
        return jax.random.normal(k_, shape, f32) * (fan_in ** -0.5)

    def gain(k_, shape):
        return 1.0 + 0.02 * jax.random.normal(k_, shape, f32)

    return {
        "x": jax.random.normal(ks[0], (BATCH, SEQ, D_MODEL), f32),
        "norm_ffn1": gain(ks[1], (DEPTH, D_MODEL)),
        "ffn1_w_gate": w(ks[2], (DEPTH, D_MODEL, D_FF), D_MODEL),
        "ffn1_w_up": w(ks[3], (DEPTH, D_MODEL, D_FF), D_MODEL),
        "ffn1_w_down": w(ks[4], (DEPTH, D_FF, D_MODEL), D_FF),
        "norm_mix": gain(ks[5], (DEPTH, D_MODEL)),
        "w_in": w(ks[6], (DEPTH, D_MODEL, IN_COLS), D_MODEL),
        "ret_norm_gain": gain(ks[7], (DEPTH, RET_WIDTH)),
        "w_out": w(ks[8], (DEPTH, D_MIX, D_MODEL), D_MIX),
        "norm_ffn2": gain(ks[9], (DEPTH, D_MODEL)),
        "ffn2_w_gate": w(ks[10], (DEPTH, D_MODEL, D_FF), D_MODEL),
        "ffn2_w_up": w(ks[11], (DEPTH, D_MODEL, D_FF), D_MODEL),
        "ffn2_w_down": w(ks[12], (DEPTH, D_FF, D_MODEL), D_FF),
        "norm_final": gain(ks[13], (D_MODEL,)),
    }


def reference(x, norm_ffn1, ffn1_w_gate, ffn1_w_up, ffn1_w_down, norm_mix, w_in, ret_norm_gain,
              w_out, norm_ffn2, ffn2_w_gate, ffn2_w_up, ffn2_w_down, norm_final):
    b, s, _ = x.shape
    h = x
    for l in range(DEPTH):
        h = h + 0.5 * swiglu(rms_norm(h, norm_ffn1[l]), ffn1_w_gate[l], ffn1_w_up[l], ffn1_w_down[l])

        u = rms_norm(h, norm_mix[l]) @ w_in[l]
        rq, rk, rv, rg, aq, ak, av = jnp.split(
            u, np.cumsum([RET_WIDTH] * 4 + [ATT_WIDTH] * 2).tolist(), axis=-1)

        rq = apply_rotary(split_heads(rq, RET_HEADS))
        rk = apply_rotary(split_heads(rk, RET_HEADS)) * (RET_HEAD_DIM ** -0.5)
        ret = chunkwise_retention(rq, rk, split_heads(rv, RET_HEADS)).astype(jnp.float32)
        mu = jnp.mean(ret, axis=-1, keepdims=True)
        var = jnp.mean(jnp.square(ret - mu), axis=-1, keepdims=True)
        ret = ((ret - mu) * lax.rsqrt(var + GN_EPS)).transpose(0, 2, 1, 3).reshape(b, s, RET_WIDTH)
        ret = (ret * ret_norm_gain[l].astype(jnp.float32)).astype(x.dtype) * jax.nn.silu(rg)

        att = dilated_attention(split_heads(aq, ATT_HEADS) * (ATT_HEAD_DIM ** -0.5),
                                split_heads(ak, ATT_HEADS), split_heads(av, ATT_HEADS))
        att = att.transpose(0, 2, 1, 3).reshape(b, s, ATT_WIDTH)

        h = h + jnp.concatenate([ret, att], axis=-1) @ w_out[l]

        h = h + 0.5 * swiglu(rms_norm(h, norm_ffn2[l]), ffn2_w_gate[l], ffn2_w_up[l], ffn2_w_down[l])
    return rms_norm(h, norm_final)
```

```python
import functools
import math

import jax
import jax.numpy as jnp
import numpy as np
from jax import lax
from jax.experimental import pallas as pl
from jax.experimental.pallas import tpu as pltpu

RET_HEADS = 4
RET_HEAD_DIM = 128
RET_WIDTH = RET_HEADS * RET_HEAD_DIM
RET_CHUNK = 128
ATT_HEADS = 8
ATT_HEAD_DIM = 64
ATT_WIDTH = ATT_HEADS * ATT_HEAD_DIM
DILATIONS = (1, 4, 16)
ATT_SPAN = 128
BAND_BLOCK = 128
ROPE_BASE = 10000.0
NORM_EPS = 1e-6
GN_EPS = 1e-6

LANES = 128
MXU_DIM = 256
VMEM_LIMIT_BYTES = 56 * 1024 * 1024

FFN_ROWS = 256
FF_CHUNK = MXU_DIM
RET_ROWS = 1024
ATT_ROWS = ATT_SPAN * DILATIONS[-1]

F32 = jnp.float32
BF16 = jnp.bfloat16


def _dot(a, b):
    return jnp.dot(a, b, preferred_element_type=F32)


def _dot_nt(a, b):
    return lax.dot_general(a, b, (((1,), (1,)), ((), ())), preferred_element_type=F32)


def _dot_tn(a, b):
    return lax.dot_general(a, b, (((0,), (0,)), ((), ())), preferred_element_type=F32)


def _rms_norm(x, gain):
    return x * lax.rsqrt(jnp.mean(x * x, axis=-1, keepdims=True) + NORM_EPS) * gain


def _swiglu_half_step(xn, wg_ref, wu_ref, wd_ref, act_ref):
    d_ff = wg_ref.shape[1]
    for c in range(d_ff // FF_CHUNK):
        cols = slice(c * FF_CHUNK, (c + 1) * FF_CHUNK)
        g = _dot(xn, wg_ref[:, cols])
        u = _dot(xn, wu_ref[:, cols])
        act_ref[:, cols] = (g * jax.nn.sigmoid(g) * u).astype(BF16)
    return _dot(act_ref[...], wd_ref[...])


def _resident(shape):
    return pl.BlockSpec(shape, lambda *_: (0,) * len(shape), pipeline_mode=pl.Buffered(1))


def _ffn_in_kernel(x_ref, g1_ref, wg_ref, wu_ref, wd_ref, gm_ref, win_ref, h_ref, u_ref, act_ref):
    x = x_ref[...]
    xn = _rms_norm(x, g1_ref[...]).astype(BF16)
    h = x + 0.5 * _swiglu_half_step(xn, wg_ref, wu_ref, wd_ref, act_ref)
    h_ref[...] = h
    hn = _rms_norm(h, gm_ref[...]).astype(BF16)
    u_ref[...] = _dot(hn, win_ref[...])


def _ffn_in(x2, g1, wg, wu, wd, gm, win):
    t, d = x2.shape
    d_ff = wg.shape[1]
    n_in = win.shape[1]
    rows = FFN_ROWS
    return pl.pallas_call(
        _ffn_in_kernel,
        grid=(t // rows,),
        in_specs=[
            pl.BlockSpec((rows, d), lambda i: (i, 0)),
            _resident((1, d)),
            _resident((d, d_ff)),
            _resident((d, d_ff)),
            _resident((d_ff, d)),
            _resident((1, d)),
            _resident((d, n_in)),
        ],
        out_specs=[
            pl.BlockSpec((rows, d), lambda i: (i, 0)),
            pl.BlockSpec((rows, n_in), lambda i: (i, 0)),
        ],
        out_shape=[
            jax.ShapeDtypeStruct((t, d), F32),
            jax.ShapeDtypeStruct((t, n_in), F32),
        ],
        scratch_shapes=[pltpu.VMEM((rows, d_ff), BF16)],
        compiler_params=pltpu.CompilerParams(
            dimension_semantics=("arbitrary",), vmem_limit_bytes=VMEM_LIMIT_BYTES),
        name="ffn_in",
    )(x2, g1, wg, wu, wd, gm, win)


def _retention_tables(seq):
    d = RET_HEAD_DIM
    pos = jnp.arange(seq, dtype=F32)
    inv_freq = ROPE_BASE ** (-jnp.arange(0, d, 2, dtype=F32) / d)
    ang = jnp.repeat(pos[:, None] * inv_freq[None, :], 2, axis=-1)
    cos = jnp.cos(ang)
    sin = jnp.sin(ang)
    even = (jnp.arange(d) % 2) == 0
    sin_signed = jnp.where(even[None, :], -sin, sin)

    c = RET_CHUNK
    log_g = jnp.log(1.0 - 2.0 ** (-5.0 - jnp.arange(RET_HEADS, dtype=F32)))
    idx = jnp.arange(c, dtype=F32)
    rel = idx[:, None] - idx[None, :]
    decay_in = jnp.where(rel >= 0, jnp.exp(log_g[:, None, None] * jnp.maximum(rel, 0.0)), 0.0)
    zeta = jnp.exp(log_g[:, None] * (c - 1 - idx)[None, :])
    xi = jnp.exp(log_g[:, None] * (idx + 1)[None, :])
    chunk_decay = jnp.exp(log_g * c)
    bcast = lambda v: jnp.broadcast_to(v[:, :, None], (RET_HEADS, c, d))
    gamma = jnp.broadcast_to(chunk_decay[:, None, None], (RET_HEADS, 8, d))
    decay = jnp.concatenate([decay_in, bcast(zeta), bcast(xi), gamma], axis=1)
    return cos, sin_signed, decay


def _retention_kernel(q_ref, k_ref, v_ref, g_ref, cos_ref, sin_ref, dec_ref, gain_ref, o_ref, state_ref):
    c = RET_CHUNK
    d = RET_HEAD_DIM

    @pl.when(pl.program_id(2) == 0)
    def _():
        state_ref[...] = jnp.zeros_like(state_ref)

    decay_in = dec_ref[0, 0:c, :]
    zeta = dec_ref[0, c:2 * c, :]
    xi = dec_ref[0, 2 * c:3 * c, :]
    gamma = dec_ref[0, 3 * c:3 * c + 1, :]
    gain = gain_ref[...]
    even_lane = (lax.broadcasted_iota(jnp.int32, (c, d), 1) % 2) == 0
    k_scale = RET_HEAD_DIM ** -0.5

    def rotary(t, cos, sin_signed):
        swapped = jnp.where(even_lane, pltpu.roll(t, d - 1, 1), pltpu.roll(t, 1, 1))
        return t * cos + swapped * sin_signed

    def chunk(i, carry):
        rows = pl.ds(pl.multiple_of(i * c, c), c)
        cos = cos_ref[rows, :]
        sin = sin_ref[rows, :]
        q = rotary(q_ref[0, rows, :], cos, sin)
        k = rotary(k_ref[0, rows, :], cos, sin) * k_scale
        vb = v_ref[0, rows, :].astype(BF16)
        qb = q.astype(BF16)
        scores = _dot_nt(qb, k.astype(BF16)) * decay_in
        intra = _dot(scores.astype(BF16), vb)
        state = state_ref[...]
        inter = _dot(qb, state.astype(BF16)) * xi
        kv = _dot_tn((k * zeta).astype(BF16), vb)
        state_ref[...] = state * gamma + kv
        ret = intra + inter
        mu = jnp.mean(ret, axis=-1, keepdims=True)
        cen = ret - mu
        var = jnp.mean(cen * cen, axis=-1, keepdims=True)
        y = cen * lax.rsqrt(var + GN_EPS) * gain
        gate = g_ref[0, rows, :]
        o_ref[0, rows, :] = (y * (gate * jax.nn.sigmoid(gate))).astype(o_ref.dtype)
        return carry

    lax.fori_loop(0, q_ref.shape[1] // c, chunk, 0)


def _retention(u3, cos, sin_signed, decay, gain):
    b, s, _ = u3.shape
    rows = RET_ROWS
    d = RET_HEAD_DIM
    h = RET_HEADS
    col = lambda off: pl.BlockSpec((1, rows, d), lambda bi, hi, si: (bi, si, off + hi))
    return pl.pallas_call(
        _retention_kernel,
        grid=(b, h, s // rows),
        in_specs=[
            col(0), col(h), col(2 * h), col(3 * h),
            pl.BlockSpec((rows, d), lambda bi, hi, si: (si, 0)),
            pl.BlockSpec((rows, d), lambda bi, hi, si: (si, 0)),
            pl.BlockSpec((1, decay.shape[1], d), lambda bi, hi, si: (hi, 0, 0)),
            pl.BlockSpec((1, d), lambda bi, hi, si: (0, hi)),
        ],
        out_specs=pl.BlockSpec((1, rows, d), lambda bi, hi, si: (bi, si, hi)),
        out_shape=jax.ShapeDtypeStruct((b, s, RET_WIDTH), BF16),
        scratch_shapes=[pltpu.VMEM((d, d), F32)],
        compiler_params=pltpu.CompilerParams(
            dimension_semantics=("arbitrary", "arbitrary", "arbitrary"), vmem_limit_bytes=VMEM_LIMIT_BYTES),
        name="retention",
    )(u3, u3, u3, u3, cos, sin_signed, decay, gain)


def _dilated_kernel(q_ref, kc_ref, kp_ref, vc_ref, vp_ref, o_ref, kbuf, vbuf, acc_ref, m_ref, l_ref):
    rows = ATT_ROWS
    blk = BAND_BLOCK
    hd = ATT_HEAD_DIM
    tile = pl.program_id(2)

    kbuf[0:rows, :] = kp_ref[0]
    kbuf[rows:2 * rows, :] = kc_ref[0]
    vbuf[0:rows, :] = vp_ref[0]
    vbuf[rows:2 * rows, :] = vc_ref[0]

    lane = lax.broadcasted_iota(jnp.int32, (blk, LANES), 1)
    head0 = lane < hd
    qi = lax.broadcasted_iota(jnp.int32, (blk, 2 * blk), 0)
    kj = lax.broadcasted_iota(jnp.int32, (blk, 2 * blk), 1)
    dist = qi + blk - kj
    band = (dist >= 0) & (dist <= ATT_SPAN)
    q_scale = ATT_HEAD_DIM ** -0.5

    def block(branch, dil, q_start, k_start, first):
        q = q_ref[0, pl.ds(q_start, blk, stride=dil), :] * q_scale
        kb = kbuf[pl.ds(k_start, 2 * blk, stride=dil), :].astype(BF16)
        vb = vbuf[pl.ds(k_start, 2 * blk, stride=dil), :].astype(BF16)
        zero = jnp.zeros_like(q)
        q2 = jnp.concatenate([jnp.where(head0, q, zero), jnp.where(head0, zero, q)], axis=0).astype(BF16)
        s = _dot_nt(q2, kb)
        valid = band & (kj >= jnp.where(first, blk, 0))
        valid2 = jnp.concatenate([valid, valid], axis=0)
        s = jnp.where(valid2, s, -jnp.inf)
        m = jnp.max(s, axis=-1, keepdims=True)
        e = jnp.exp(s - m)
        l = jnp.sum(e, axis=-1, keepdims=True)
        pv = _dot(e.astype(BF16), vb)
        pair = lambda t: jnp.where(head0, jnp.broadcast_to(t[0:blk], (blk, LANES)),
                                   jnp.broadcast_to(t[blk:2 * blk], (blk, LANES)))
        dst = pl.ds(q_start, blk, stride=dil)
        acc_ref[branch, dst, :] = pair(pv)
        m_ref[branch, dst, :] = pair(m)
        l_ref[branch, dst, :] = pair(l)

    for branch, dil in enumerate(DILATIONS):
        per_res = rows // dil // blk

        def body(i, carry, branch=branch, dil=dil, per_res=per_res):
            res = i // per_res
            jb = i % per_res
            q_start = res + dil * blk * jb
            k_start = rows + res + dil * blk * (jb - 1)
            block(branch, dil, q_start, k_start, (tile == 0) & (jb == 0))
            return carry

        lax.fori_loop(0, dil * per_res, body, 0)

    def combine(i, carry):
        r = pl.ds(pl.multiple_of(i * blk, blk), blk)
        m0, m1, m2 = m_ref[0, r, :], m_ref[1, r, :], m_ref[2, r, :]
        mx = jnp.maximum(jnp.maximum(m0, m1), m2)
        w0, w1, w2 = jnp.exp(m0 - mx), jnp.exp(m1 - mx), jnp.exp(m2 - mx)
        num = w0 * acc_ref[0, r, :] + w1 * acc_ref[1, r, :] + w2 * acc_ref[2, r, :]
        den = w0 * l_ref[0, r, :] + w1 * l_ref[1, r, :] + w2 * l_ref[2, r, :]
        o_ref[0, r, :] = (num / den).astype(o_ref.dtype)
        return carry

    lax.fori_loop(0, rows // blk, combine, 0)


def _dilated(u3, col0):
    b, s, _ = u3.shape
    rows = ATT_ROWS
    pairs = ATT_WIDTH // LANES
    cur = lambda off: pl.BlockSpec((1, rows, LANES), lambda bi, pi, ti: (bi, ti, col0 + off + pi))
    prev = lambda off: pl.BlockSpec((1, rows, LANES),
                                    lambda bi, pi, ti: (bi, jnp.maximum(ti - 1, 0), col0 + off + pi))
    return pl.pallas_call(
        _dilated_kernel,
        grid=(b, pairs, s // rows),
        in_specs=[cur(0), cur(pairs), prev(pairs), cur(2 * pairs), prev(2 * pairs)],
        out_specs=pl.BlockSpec((1, rows, LANES), lambda bi, pi, ti: (bi, ti, pi)),
        out_shape=jax.ShapeDtypeStruct((b, s, ATT_WIDTH), BF16),
        scratch_shapes=[
            pltpu.VMEM((2 * rows, LANES), F32),
            pltpu.VMEM((2 * rows, LANES), F32),
            pltpu.VMEM((len(DILATIONS), rows, LANES), F32),
            pltpu.VMEM((len(DILATIONS), rows, LANES), F32),
            pltpu.VMEM((len(DILATIONS), rows, LANES), F32),
        ],
        compiler_params=pltpu.CompilerParams(
            dimension_semantics=("arbitrary", "arbitrary", "arbitrary"), vmem_limit_bytes=VMEM_LIMIT_BYTES),
        name="dilated",
    )(u3, u3, u3, u3, u3)


def _out_ffn_kernel(h_ref, ret_ref, att_ref, wo_ref, g2_ref, wg_ref, wu_ref, wd_ref, gf_ref, y_ref, act_ref,
                    *, final_norm):
    mix = jnp.concatenate([ret_ref[...], att_ref[...]], axis=-1)
    h = h_ref[...] + _dot(mix, wo_ref[...])
    hn = _rms_norm(h, g2_ref[...]).astype(BF16)
    h = h + 0.5 * _swiglu_half_step(hn, wg_ref, wu_ref, wd_ref, act_ref)
    if final_norm:
        h = _rms_norm(h, gf_ref[...])
    y_ref[...] = h


def _out_ffn(h1, ret, att, wo, g2, wg, wu, wd, gf, final_norm):
    t, d = h1.shape
    d_ff = wg.shape[1]
    rows = FFN_ROWS
    return pl.pallas_call(
        functools.partial(_out_ffn_kernel, final_norm=final_norm),
        grid=(t // rows,),
        in_specs=[
            pl.BlockSpec((rows, d), lambda i: (i, 0)),
            pl.BlockSpec((rows, ret.shape[1]), lambda i: (i, 0)),
            pl.BlockSpec((rows, att.shape[1]), lambda i: (i, 0)),
            _resident(wo.shape),
            _resident((1, d)),
            _resident((d, d_ff)),
            _resident((d, d_ff)),
            _resident((d_ff, d)),
            _resident((1, d)),
        ],
        out_specs=pl.BlockSpec((rows, d), lambda i: (i, 0)),
        out_shape=jax.ShapeDtypeStruct((t, d), F32),
        scratch_shapes=[pltpu.VMEM((rows, d_ff), BF16)],
        compiler_params=pltpu.CompilerParams(
            dimension_semantics=("arbitrary",), vmem_limit_bytes=VMEM_LIMIT_BYTES),
        name="out_ffn",
    )(h1, ret, att, wo, g2, wg, wu, wd, gf)


def kernel(x, norm_ffn1, ffn1_w_gate, ffn1_w_up, ffn1_w_down, norm_mix, w_in, ret_norm_gain,
           w_out, norm_ffn2, ffn2_w_gate, ffn2_w_up, ffn2_w_down, norm_final):
    b, s, d = x.shape
    depth = norm_ffn1.shape[0]
    assert s % ATT_ROWS == 0 and s % RET_ROWS == 0 and (b * s) % FFN_ROWS == 0
    assert w_in.shape[2] == 4 * RET_WIDTH + 3 * ATT_WIDTH
    cos, sin_signed, decay = _retention_tables(s)
    row = lambda v: v.reshape(1, -1)
    bf = lambda w: w.astype(BF16)

    h = x.reshape(b * s, d)
    for l in range(depth):
        h1, u = _ffn_in(h, row(norm_ffn1[l]), bf(ffn1_w_gate[l]), bf(ffn1_w_up[l]), bf(ffn1_w_down[l]),
                        row(norm_mix[l]), bf(w_in[l]))
        u3 = u.reshape(b, s, -1)
        ret = _retention(u3, cos, sin_signed, decay, row(ret_norm_gain[l]))
        att = _dilated(u3, 4 * RET_WIDTH // LANES)
        h = _out_ffn(h1, ret.reshape(b * s, -1), att.reshape(b * s, -1), bf(w_out[l]), row(norm_ffn2[l]),
                     bf(ffn2_w_gate[l]), bf(ffn2_w_up[l]), bf(ffn2_w_down[l]), row(norm_final),
                     final_norm=(l == depth - 1))
    return h.reshape(b, s, d)
```

```python
import functools

import jax
import jax.numpy as jnp
import numpy as np
from jax import lax
from jax.experimental import pallas as pl
from jax.experimental.pallas import tpu as pltpu

RET_HEADS = 4
RET_HEAD_DIM = 128
RET_WIDTH = RET_HEADS * RET_HEAD_DIM
RET_CHUNK = 128
ATT_HEADS = 8
ATT_HEAD_DIM = 64
ATT_WIDTH = ATT_HEADS * ATT_HEAD_DIM
DILATIONS = (1, 4, 16)
ATT_SPAN = 128
BAND_BLOCK = 128
ROPE_BASE = 10000.0
NORM_EPS = 1e-6
GN_EPS = 1e-6

LANES = 128
MXU_DIM = 256
VMEM_LIMIT_BYTES = 56 * 1024 * 1024

FFN_ROWS = 256
FF_CHUNK = MXU_DIM
RET_ROWS = 1024
ATT_ROWS = ATT_SPAN * DILATIONS[-1]

F32 = jnp.float32
BF16 = jnp.bfloat16


def _dot(a, b):
    return jnp.dot(a, b, preferred_element_type=F32)


def _dot_nt(a, b):
    return lax.dot_general(a, b, (((1,), (1,)), ((), ())), preferred_element_type=F32)


def _dot_tn(a, b):
    return lax.dot_general(a, b, (((0,), (0,)), ((), ())), preferred_element_type=F32)


def _rms_norm(x, gain):
    return x * lax.rsqrt(jnp.mean(x * x, axis=-1, keepdims=True) + NORM_EPS) * gain


def _swiglu_half_step(xn_ref, wg_ref, wu_ref, wd_ref, act_ref):
    d_ff = wg_ref.shape[1]
    for c in range(d_ff // FF_CHUNK):
        cols = slice(c * FF_CHUNK, (c + 1) * FF_CHUNK)
        g = _dot(xn_ref[...], wg_ref[:, cols])
        u = _dot(xn_ref[...], wu_ref[:, cols])
        act_ref[:, cols] = (g * jax.nn.sigmoid(g) * u).astype(BF16)
    return _dot(act_ref[...], wd_ref[...])


def _resident(shape):
    return pl.BlockSpec(shape, lambda *_: (0,) * len(shape), pipeline_mode=pl.Buffered(1))


def _ffn_in_kernel(x_ref, g1_ref, wg_ref, wu_ref, wd_ref, gm_ref, win_ref,
                   h_ref, ur_ref, a1_ref, a4_ref, a16_ref, xn_ref, act_ref, slab_ref, by4_ref):
    rows = x_ref.shape[0]
    n_ret = ur_ref.shape[1]
    x = x_ref[...]
    xn_ref[...] = _rms_norm(x, g1_ref[...]).astype(BF16)
    h = x + 0.5 * _swiglu_half_step(xn_ref, wg_ref, wu_ref, wd_ref, act_ref)
    h_ref[...] = h
    xn_ref[...] = _rms_norm(h, gm_ref[...]).astype(BF16)
    for c in range(slab_ref.shape[0] * LANES // MXU_DIM):
        cols = slice(c * MXU_DIM, (c + 1) * MXU_DIM)
        u = _dot(xn_ref[...], win_ref[:, n_ret + c * MXU_DIM:n_ret + (c + 1) * MXU_DIM])
        if (c + 1) * MXU_DIM <= ATT_WIDTH:
            u = u * (ATT_HEAD_DIM ** -0.5)
        a1_ref[:, cols] = u.astype(BF16)
        d4, d16 = DILATIONS[1], DILATIONS[2]
        for half in range(MXU_DIM // LANES):
            lanes = slice(c * MXU_DIM + half * LANES, c * MXU_DIM + (half + 1) * LANES)
            slab = slab_ref.at[c * (MXU_DIM // LANES) + half]
            by4 = by4_ref.at[c * (MXU_DIM // LANES) + half]
            slab[...] = u[:, half * LANES:(half + 1) * LANES]
            for a in range(d4):
                by4[a] = slab[pl.ds(a, rows // d4, stride=d4), :]
                a4_ref[0, a, :, lanes] = by4[a].astype(BF16)
            for res in range(d16):
                a16_ref[0, res, :, lanes] = (
                    by4[res % d4, pl.ds(res // d4, rows // d16, stride=d16 // d4), :].astype(BF16))
    ur_ref[...] = _dot(xn_ref[...], win_ref[:, 0:n_ret])


def _ffn_in(x2, g1, wg, wu, wd, gm, win, batch):
    t, d = x2.shape
    d_ff = wg.shape[1]
    n_ret = 4 * RET_WIDTH
    n_att = 3 * ATT_WIDTH
    rows = FFN_ROWS
    seq = t // batch
    per_seq = seq // rows
    d4, d16 = DILATIONS[1], DILATIONS[2]
    regroup = lambda dil: pl.BlockSpec((1, dil, rows // dil, n_att),
                                       lambda i: (i // per_seq, 0, i % per_seq, 0))
    return pl.pallas_call(
        _ffn_in_kernel,
        grid=(t // rows,),
        in_specs=[
            pl.BlockSpec((rows, d), lambda i: (i, 0)),
            _resident((1, d)),
            _resident((d, d_ff)),
            _resident((d, d_ff)),
            _resident((d_ff, d)),
            _resident((1, d)),
            _resident((d, n_ret + n_att)),
        ],
        out_specs=[
            pl.BlockSpec((rows, d), lambda i: (i, 0)),
            pl.BlockSpec((rows, n_ret), lambda i: (i, 0)),
            pl.BlockSpec((rows, n_att), lambda i: (i, 0)),
            regroup(d4),
            regroup(d16),
        ],
        out_shape=[
            jax.ShapeDtypeStruct((t, d), F32),
            jax.ShapeDtypeStruct((t, n_ret), F32),
            jax.ShapeDtypeStruct((t, n_att), BF16),
            jax.ShapeDtypeStruct((batch, d4, seq // d4, n_att), BF16),
            jax.ShapeDtypeStruct((batch, d16, seq // d16, n_att), BF16),
        ],
        scratch_shapes=[pltpu.VMEM((rows, d), BF16), pltpu.VMEM((rows, d_ff), BF16),
                        pltpu.VMEM((n_att // LANES, rows, LANES), F32),
                        pltpu.VMEM((n_att // LANES, d4, rows // d4, LANES), F32)],
        compiler_params=pltpu.CompilerParams(
            dimension_semantics=("arbitrary",), vmem_limit_bytes=VMEM_LIMIT_BYTES),
        name="ffn_in",
    )(x2, g1, wg, wu, wd, gm, win)


def _retention_tables(seq):
    d = RET_HEAD_DIM
    pos = jnp.arange(seq, dtype=F32)
    inv_freq = ROPE_BASE ** (-jnp.arange(0, d, 2, dtype=F32) / d)
    ang = jnp.repeat(pos[:, None] * inv_freq[None, :], 2, axis=-1)
    cos = jnp.cos(ang)
    sin = jnp.sin(ang)
    even = (jnp.arange(d) % 2) == 0
    sin_signed = jnp.where(even[None, :], -sin, sin)

    c = RET_CHUNK
    log_g = jnp.log(1.0 - 2.0 ** (-5.0 - jnp.arange(RET_HEADS, dtype=F32)))
    idx = jnp.arange(c, dtype=F32)
    rel = idx[:, None] - idx[None, :]
    decay_in = jnp.where(rel >= 0, jnp.exp(log_g[:, None, None] * jnp.maximum(rel, 0.0)), 0.0)
    zeta = jnp.exp(log_g[:, None] * (c - 1 - idx)[None, :])
    xi = jnp.exp(log_g[:, None] * (idx + 1)[None, :])
    chunk_decay = jnp.exp(log_g * c)
    bcast = lambda v: jnp.broadcast_to(v[:, :, None], (RET_HEADS, c, d))
    gamma = jnp.broadcast_to(chunk_decay[:, None, None], (RET_HEADS, 8, d))
    decay = jnp.concatenate([decay_in, bcast(zeta), bcast(xi), gamma], axis=1)
    return cos, sin_signed, decay


def _retention_kernel(q_ref, k_ref, v_ref, g_ref, cos_ref, sin_ref, dec_ref, gain_ref, o_ref, state_ref):
    c = RET_CHUNK
    d = RET_HEAD_DIM

    @pl.when(pl.program_id(2) == 0)
    def _():
        state_ref[...] = jnp.zeros_like(state_ref)

    decay_in = dec_ref[0, 0:c, :]
    zeta = dec_ref[0, c:2 * c, :]
    xi = dec_ref[0, 2 * c:3 * c, :]
    gamma = dec_ref[0, 3 * c:3 * c + 1, :]
    gain = gain_ref[...]
    even_lane = (lax.broadcasted_iota(jnp.int32, (c, d), 1) % 2) == 0
    k_scale = RET_HEAD_DIM ** -0.5

    def rotary(t, cos, sin_signed):
        swapped = jnp.where(even_lane, pltpu.roll(t, d - 1, 1), pltpu.roll(t, 1, 1))
        return t * cos + swapped * sin_signed

    def chunk(i, carry):
        rows = pl.ds(pl.multiple_of(i * c, c), c)
        cos = cos_ref[rows, :]
        sin = sin_ref[rows, :]
        q = rotary(q_ref[0, rows, :], cos, sin)
        k = rotary(k_ref[0, rows, :], cos, sin) * k_scale
        vb = v_ref[0, rows, :].astype(BF16)
        qb = q.astype(BF16)
        scores = _dot_nt(qb, k.astype(BF16)) * decay_in
        intra = _dot(scores.astype(BF16), vb)
        state = state_ref[...]
        inter = _dot(qb, state.astype(BF16)) * xi
        kv = _dot_tn((k * zeta).astype(BF16), vb)
        state_ref[...] = state * gamma + kv
        ret = intra + inter
        mu = jnp.mean(ret, axis=-1, keepdims=True)
        cen = ret - mu
        var = jnp.mean(cen * cen, axis=-1, keepdims=True)
        y = cen * lax.rsqrt(var + GN_EPS) * gain
        gate = g_ref[0, rows, :]
        o_ref[0, rows, :] = (y * (gate * jax.nn.sigmoid(gate))).astype(o_ref.dtype)
        return carry

    lax.fori_loop(0, q_ref.shape[1] // c, chunk, 0)


def _retention(u3, cos, sin_signed, decay, gain):
    b, s, _ = u3.shape
    rows = RET_ROWS
    d = RET_HEAD_DIM
    h = RET_HEADS
    col = lambda off: pl.BlockSpec((1, rows, d), lambda bi, hi, si: (bi, si, off + hi))
    return pl.pallas_call(
        _retention_kernel,
        grid=(b, h, s // rows),
        in_specs=[
            col(0), col(h), col(2 * h), col(3 * h),
            pl.BlockSpec((rows, d), lambda bi, hi, si: (si, 0)),
            pl.BlockSpec((rows, d), lambda bi, hi, si: (si, 0)),
            pl.BlockSpec((1, decay.shape[1], d), lambda bi, hi, si: (hi, 0, 0)),
            pl.BlockSpec((1, d), lambda bi, hi, si: (0, hi)),
        ],
        out_specs=pl.BlockSpec((1, rows, d), lambda bi, hi, si: (bi, si, hi)),
        out_shape=jax.ShapeDtypeStruct((b, s, RET_WIDTH), BF16),
        scratch_shapes=[pltpu.VMEM((d, d), F32)],
        compiler_params=pltpu.CompilerParams(
            dimension_semantics=("arbitrary", "arbitrary", "arbitrary"), vmem_limit_bytes=VMEM_LIMIT_BYTES),
        name="retention",
    )(u3, u3, u3, u3, cos, sin_signed, decay, gain)


ATT_BLOCKS = ATT_ROWS // BAND_BLOCK
KIND_ACC, KIND_M, KIND_L = 0, 1, 2
N_KINDS = 3


def _attention_bias():
    blk = BAND_BLOCK
    qi = np.arange(blk)[:, None]
    kj = np.arange(2 * blk)[None, :]
    dist = qi + blk - kj
    band = (dist >= 0) & (dist <= ATT_SPAN)
    masks = np.stack([band, band & (kj >= blk)])
    return jnp.asarray(np.where(masks, 0.0, -np.inf).astype(np.float32))


def _dilated_kernel(q1, k1c, k1p, v1c, v1p, q4, k4c, k4p, v4c, v4p, q16, k16c, k16p, v16c, v16p, bias_ref,
                    o_ref, kb0, kb1, kb2, vb0, vb1, vb2, s_ref, e_ref, res_ref, y_ref):
    rows = ATT_ROWS
    blk = BAND_BLOCK
    hd = ATT_HEAD_DIM
    tile = pl.program_id(2)
    n_br = len(DILATIONS)
    kbufs, vbufs = (kb0, kb1, kb2), (vb0, vb1, vb2)
    d4, d16 = DILATIONS[1], DILATIONS[2]
    log2 = lambda n: n.bit_length() - 1

    ones = jnp.ones((blk, LANES), BF16)
    per_branch = (
        ([k1p.at[0]], [k1c.at[0]], [v1p.at[0]], [v1c.at[0]]),
        ([k4p.at[0, r] for r in range(d4)], [k4c.at[0, r] for r in range(d4)],
         [v4p.at[0, r] for r in range(d4)], [v4c.at[0, r] for r in range(d4)]),
        ([k16p.at[0, r] for r in range(d16)], [k16c.at[0, r] for r in range(d16)],
         [v16p.at[0, r] for r in range(d16)], [v16c.at[0, r] for r in range(d16)]),
    )
    for br, (kps, kcs, vps, vcs) in enumerate(per_branch):
        cur = rows // len(kps)
        for r in range(len(kps)):
            base = r * (blk + cur)
            kbufs[br][base:base + blk, :] = kps[r][...]
            kbufs[br][base + blk:base + blk + cur, :] = kcs[r][...]
            vbufs[br][base:base + blk, 0:LANES] = vps[r][...]
            vbufs[br][base + blk:base + blk + cur, 0:LANES] = vcs[r][...]
        for j in range(vbufs[br].shape[0] // blk):
            vbufs[br][j * blk:(j + 1) * blk, LANES:2 * LANES] = ones

    head0 = lax.broadcasted_iota(jnp.int32, (blk, LANES), 1) < hd

    def residue(br, idx):
        return (0, idx >> log2(ATT_BLOCKS // d4), idx)[br]

    def is_first(br, idx):
        jb = (idx, idx & (ATT_BLOCKS // d4 - 1), 0)[br]
        return ((tile == 0) & (jb == 0)).astype(jnp.int32)

    def key_rows(br, idx):
        return pl.ds(pl.multiple_of((idx + residue(br, idx)) * blk, blk), 2 * blk)

    def load_q(br, idx):
        if br == 0:
            return q1[0, pl.ds(pl.multiple_of(idx * blk, blk), blk), :]
        if br == 1:
            jb = idx & (ATT_BLOCKS // d4 - 1)
            return q4[0, residue(br, idx), pl.ds(pl.multiple_of(jb * blk, blk), blk), :]
        return q16[0, idx]

    def store_result(br, idx, kind, val):
        if br == 0:
            res_ref[kind, pl.ds(pl.multiple_of(idx * blk, blk), blk), :] = val
        elif br == 1:
            jb = idx & (ATT_BLOCKS // d4 - 1)
            res_ref[N_KINDS + kind, pl.ds(residue(br, idx) + d4 * blk * jb, blk, stride=d4), :] = val
        else:
            a = idx & (d4 - 1)
            b = idx >> log2(d4)
            y_ref[kind, pl.ds(a * (rows // d4) + b, blk, stride=d4), :] = val

    def pair(lo, hi):
        return jnp.where(head0, jnp.broadcast_to(lo, (blk, LANES)), jnp.broadcast_to(hi, (blk, LANES)))

    def scores(idx):
        for br in range(n_br):
            q = load_q(br, idx)
            zero = jnp.zeros_like(q)
            q2 = jnp.concatenate([jnp.where(head0, q, zero), jnp.where(head0, zero, q)], axis=0)
            s_ref[br] = _dot_nt(q2, kbufs[br][key_rows(br, idx), :])

    def softmax(idx):
        for br in range(n_br):
            bias = bias_ref[is_first(br, idx)]
            maxima = []
            for h in range(2):
                r = slice(h * blk, (h + 1) * blk)
                s = s_ref[br, r, :] + bias
                m = jnp.max(s, axis=-1, keepdims=True)
                e_ref[br, r, :] = jnp.exp(s - m).astype(BF16)
                maxima.append(m)
            store_result(br, idx, KIND_M, pair(*maxima))

    def values(idx):
        for br in range(n_br):
            pv = _dot(e_ref[br], vbufs[br][key_rows(br, idx), :])
            store_result(br, idx, KIND_ACC, pair(pv[0:blk, 0:LANES], pv[blk:2 * blk, 0:LANES]))
            store_result(br, idx, KIND_L, pair(pv[0:blk, LANES:2 * LANES], pv[blk:2 * blk, LANES:2 * LANES]))

    scores(0)
    softmax(0)
    scores(1)

    def steady(i, carry):
        values(i)
        softmax(i + 1)
        scores(i + 2)
        return carry

    lax.fori_loop(0, ATT_BLOCKS - 2, steady, 0, unroll=2)
    values(ATT_BLOCKS - 2)
    softmax(ATT_BLOCKS - 1)
    values(ATT_BLOCKS - 1)

    sub = rows // d4
    for kind in range(N_KINDS):
        for a in range(d4):
            for c in range(sub // blk):
                res_ref[2 * N_KINDS + kind, pl.ds(a + d4 * blk * c, blk, stride=d4), :] = (
                    y_ref[kind, a * sub + c * blk:a * sub + (c + 1) * blk, :])

    def combine(i, carry):
        r = pl.ds(pl.multiple_of(i * blk, blk), blk)
        ms = [res_ref[br * N_KINDS + KIND_M, r, :] for br in range(n_br)]
        mx = jnp.maximum(jnp.maximum(ms[0], ms[1]), ms[2])
        ws = [jnp.exp(m - mx) for m in ms]
        num = sum(w * res_ref[br * N_KINDS + KIND_ACC, r, :] for br, w in enumerate(ws))
        den = sum(w * res_ref[br * N_KINDS + KIND_L, r, :] for br, w in enumerate(ws))
        o_ref[0, r, :] = (num / den).astype(o_ref.dtype)
        return carry

    lax.fori_loop(0, rows // blk, combine, 0)


def _dilated(a1, a4, a16, bias):
    b, s, _ = a1.shape
    rows = ATT_ROWS
    blk = BAND_BLOCK
    pairs = ATT_WIDTH // LANES
    d4, d16 = DILATIONS[1], DILATIONS[2]
    prev_of = lambda n: (lambda i: jnp.maximum(i * n - 1, 0))

    def specs(dil):
        sub = rows // dil
        per = sub // blk
        if dil == 1:
            cur = lambda off: pl.BlockSpec((1, sub, LANES), lambda bi, pi, ti: (bi, ti, off + pi))
            prv = lambda off: pl.BlockSpec((1, blk, LANES),
                                           lambda bi, pi, ti: (bi, prev_of(per)(ti), off + pi))
        else:
            cur = lambda off: pl.BlockSpec((1, dil, sub, LANES), lambda bi, pi, ti: (bi, 0, ti, off + pi))
            prv = lambda off: pl.BlockSpec((1, dil, blk, LANES),
                                           lambda bi, pi, ti: (bi, 0, prev_of(per)(ti), off + pi))
        return [cur(0), cur(pairs), prv(pairs), cur(2 * pairs), prv(2 * pairs)]

    key_rows = lambda dil: (ATT_BLOCKS + dil) * blk
    return pl.pallas_call(
        _dilated_kernel,
        grid=(b, pairs, s // rows),
        in_specs=specs(1) + specs(d4) + specs(d16) + [_resident(bias.shape)],
        out_specs=pl.BlockSpec((1, rows, LANES), lambda bi, pi, ti: (bi, ti, pi)),
        out_shape=jax.ShapeDtypeStruct((b, s, ATT_WIDTH), BF16),
        scratch_shapes=(
            [pltpu.VMEM((key_rows(dil), LANES), BF16) for dil in DILATIONS]
            + [pltpu.VMEM((key_rows(dil), 2 * LANES), BF16) for dil in DILATIONS]
            + [pltpu.VMEM((len(DILATIONS), 2 * blk, 2 * blk), F32),
               pltpu.VMEM((len(DILATIONS), 2 * blk, 2 * blk), BF16),
               pltpu.VMEM((len(DILATIONS) * N_KINDS, rows, LANES), F32),
               pltpu.VMEM((N_KINDS, rows, LANES), F32)]),
        compiler_params=pltpu.CompilerParams(
            dimension_semantics=("arbitrary", "arbitrary", "arbitrary"), vmem_limit_bytes=VMEM_LIMIT_BYTES),
        name="dilated",
    )(a1, a1, a1, a1, a1, a4, a4, a4, a4, a4, a16, a16, a16, a16, a16, bias)


def _out_ffn_kernel(h_ref, ret_ref, att_ref, wo_ref, g2_ref, wg_ref, wu_ref, wd_ref, gf_ref, y_ref,
                    xn_ref, act_ref, *, final_norm):
    mix = jnp.concatenate([ret_ref[...], att_ref[...]], axis=-1)
    h = h_ref[...] + _dot(mix, wo_ref[...])
    xn_ref[...] = _rms_norm(h, g2_ref[...]).astype(BF16)
    h = h + 0.5 * _swiglu_half_step(xn_ref, wg_ref, wu_ref, wd_ref, act_ref)
    if final_norm:
        h = _rms_norm(h, gf_ref[...])
    y_ref[...] = h


def _out_ffn(h1, ret, att, wo, g2, wg, wu, wd, gf, final_norm):
    t, d = h1.shape
    d_ff = wg.shape[1]
    rows = FFN_ROWS
    return pl.pallas_call(
        functools.partial(_out_ffn_kernel, final_norm=final_norm),
        grid=(t // rows,),
        in_specs=[
            pl.BlockSpec((rows, d), lambda i: (i, 0)),
            pl.BlockSpec((rows, ret.shape[1]), lambda i: (i, 0)),
            pl.BlockSpec((rows, att.shape[1]), lambda i: (i, 0)),
            _resident(wo.shape),
            _resident((1, d)),
            _resident((d, d_ff)),
            _resident((d, d_ff)),
            _resident((d_ff, d)),
            _resident((1, d)),
        ],
        out_specs=pl.BlockSpec((rows, d), lambda i: (i, 0)),
        out_shape=jax.ShapeDtypeStruct((t, d), F32),
        scratch_shapes=[pltpu.VMEM((rows, d), BF16), pltpu.VMEM((rows, d_ff), BF16)],
        compiler_params=pltpu.CompilerParams(
            dimension_semantics=("arbitrary",), vmem_limit_bytes=VMEM_LIMIT_BYTES),
        name="out_ffn",
    )(h1, ret, att, wo, g2, wg, wu, wd, gf)


def kernel(x, norm_ffn1, ffn1_w_gate, ffn1_w_up, ffn1_w_down, norm_mix, w_in, ret_norm_gain,
           w_out, norm_ffn2, ffn2_w_gate, ffn2_w_up, ffn2_w_down, norm_final):
    b, s, d = x.shape
    depth = norm_ffn1.shape[0]
    assert s % ATT_ROWS == 0 and s % RET_ROWS == 0 and (b * s) % FFN_ROWS == 0
    assert w_in.shape[2] == 4 * RET_WIDTH + 3 * ATT_WIDTH
    cos, sin_signed, decay = _retention_tables(s)
    bias = _attention_bias()
    row = lambda v: v.reshape(1, -1)
    bf = lambda w: w.astype(BF16)

    h = x.reshape(b * s, d)
    for l in range(depth):
        h1, u_ret, a1, a4, a16 = _ffn_in(
            h, row(norm_ffn1[l]), bf(ffn1_w_gate[l]), bf(ffn1_w_up[l]), bf(ffn1_w_down[l]),
            row(norm_mix[l]), bf(w_in[l]), b)
        ret = _retention(u_ret.reshape(b, s, -1), cos, sin_signed, decay, row(ret_norm_gain[l]))
        att = _dilated(a1.reshape(b, s, -1), a4, a16, bias)
        h = _out_ffn(h1, ret.reshape(b * s, -1), att.reshape(b * s, -1), bf(w_out[l]), row(norm_ffn2[l]),
                     bf(ffn2_w_gate[l]), bf(ffn2_w_up[l]), bf(ffn2_w_down[l]), row(norm_final),
                     final_norm=(l == depth - 1))
    return h.reshape(b, s, d)
```

```python
import functools

import jax
import jax.numpy as jnp
import numpy as np
from jax import lax
from jax.experimental import pallas as pl
from jax.experimental.pallas import tpu as pltpu

RET_HEADS = 4
RET_HEAD_DIM = 128
RET_WIDTH = RET_HEADS * RET_HEAD_DIM
ATT_HEADS = 8
ATT_HEAD_DIM = 64
ATT_WIDTH = ATT_HEADS * ATT_HEAD_DIM
DILATIONS = (1, 4, 16)
ATT_SPAN = 128
BAND_BLOCK = 128
ROPE_BASE = 10000.0
NORM_EPS = 1e-6
GN_EPS = 1e-6

LANES = 128
MXU_DIM = 256
VMEM_LIMIT_BYTES = 56 * 1024 * 1024

FFN_ROWS = 256
FF_CHUNK = MXU_DIM
RET_ROWS = 1024
RET_CHUNK = MXU_DIM
ATT_ROWS = ATT_SPAN * DILATIONS[-1]

F32 = jnp.float32
BF16 = jnp.bfloat16


def _dot(a, b):
    return jnp.dot(a, b, preferred_element_type=F32)


def _dot_nt(a, b):
    return lax.dot_general(a, b, (((1,), (1,)), ((), ())), preferred_element_type=F32)


def _dot_tn(a, b):
    return lax.dot_general(a, b, (((0,), (0,)), ((), ())), preferred_element_type=F32)


def _rms_norm(x, gain):
    return x * lax.rsqrt(jnp.mean(x * x, axis=-1, keepdims=True) + NORM_EPS) * gain


def _swiglu_half_step(xn_ref, wg_ref, wu_ref, wd_ref, act_ref):
    d_ff = wg_ref.shape[1]
    for c in range(d_ff // FF_CHUNK):
        cols = slice(c * FF_CHUNK, (c + 1) * FF_CHUNK)
        g = _dot(xn_ref[...], wg_ref[:, cols])
        u = _dot(xn_ref[...], wu_ref[:, cols])
        act_ref[:, cols] = (g * jax.nn.sigmoid(g) * u).astype(BF16)
    return _dot(act_ref[...], wd_ref[...])


def _resident(shape):
    return pl.BlockSpec(shape, lambda *_: (0,) * len(shape), pipeline_mode=pl.Buffered(1))


def _ffn_in_kernel(x_ref, g1_ref, wg_ref, wu_ref, wd_ref, gm_ref, win_ref,
                   h_ref, ur_ref, a1_ref, a4_ref, a16_ref, xn_ref, act_ref, slab_ref, by4_ref):
    rows = x_ref.shape[0]
    n_ret = ur_ref.shape[1]
    x = x_ref[...]
    xn_ref[...] = _rms_norm(x, g1_ref[...]).astype(BF16)
    h = x + 0.5 * _swiglu_half_step(xn_ref, wg_ref, wu_ref, wd_ref, act_ref)
    h_ref[...] = h
    xn_ref[...] = _rms_norm(h, gm_ref[...]).astype(BF16)
    for c in range(slab_ref.shape[0] * LANES // MXU_DIM):
        cols = slice(c * MXU_DIM, (c + 1) * MXU_DIM)
        u = _dot(xn_ref[...], win_ref[:, n_ret + c * MXU_DIM:n_ret + (c + 1) * MXU_DIM])
        if (c + 1) * MXU_DIM <= ATT_WIDTH:
            u = u * (ATT_HEAD_DIM ** -0.5)
        a1_ref[:, cols] = u.astype(BF16)
        d4, d16 = DILATIONS[1], DILATIONS[2]
        for half in range(MXU_DIM // LANES):
            lanes = slice(c * MXU_DIM + half * LANES, c * MXU_DIM + (half + 1) * LANES)
            slab = slab_ref.at[c * (MXU_DIM // LANES) + half]
            by4 = by4_ref.at[c * (MXU_DIM // LANES) + half]
            slab[...] = u[:, half * LANES:(half + 1) * LANES]
            for a in range(d4):
                by4[a] = slab[pl.ds(a, rows // d4, stride=d4), :]
                a4_ref[0, a, :, lanes] = by4[a].astype(BF16)
            for res in range(d16):
                a16_ref[0, res, :, lanes] = (
                    by4[res % d4, pl.ds(res // d4, rows // d16, stride=d16 // d4), :].astype(BF16))
    ur_ref[...] = _dot(xn_ref[...], win_ref[:, 0:n_ret])


def _ffn_in(x2, g1, wg, wu, wd, gm, win, batch):
    t, d = x2.shape
    d_ff = wg.shape[1]
    n_ret = 4 * RET_WIDTH
    n_att = 3 * ATT_WIDTH
    rows = FFN_ROWS
    seq = t // batch
    per_seq = seq // rows
    d4, d16 = DILATIONS[1], DILATIONS[2]
    regroup = lambda dil: pl.BlockSpec((1, dil, rows // dil, n_att),
                                       lambda i: (i // per_seq, 0, i % per_seq, 0))
    return pl.pallas_call(
        _ffn_in_kernel,
        grid=(t // rows,),
        in_specs=[
            pl.BlockSpec((rows, d), lambda i: (i, 0)),
            _resident((1, d)),
            _resident((d, d_ff)),
            _resident((d, d_ff)),
            _resident((d_ff, d)),
            _resident((1, d)),
            _resident((d, n_ret + n_att)),
        ],
        out_specs=[
            pl.BlockSpec((rows, d), lambda i: (i, 0)),
            pl.BlockSpec((rows, n_ret), lambda i: (i, 0)),
            pl.BlockSpec((rows, n_att), lambda i: (i, 0)),
            regroup(d4),
            regroup(d16),
        ],
        out_shape=[
            jax.ShapeDtypeStruct((t, d), F32),
            jax.ShapeDtypeStruct((t, n_ret), F32),
            jax.ShapeDtypeStruct((t, n_att), BF16),
            jax.ShapeDtypeStruct((batch, d4, seq // d4, n_att), BF16),
            jax.ShapeDtypeStruct((batch, d16, seq // d16, n_att), BF16),
        ],
        scratch_shapes=[pltpu.VMEM((rows, d), BF16), pltpu.VMEM((rows, d_ff), BF16),
                        pltpu.VMEM((n_att // LANES, rows, LANES), F32),
                        pltpu.VMEM((n_att // LANES, d4, rows // d4, LANES), F32)],
        compiler_params=pltpu.CompilerParams(
            dimension_semantics=("arbitrary",), vmem_limit_bytes=VMEM_LIMIT_BYTES),
        name="ffn_in",
    )(x2, g1, wg, wu, wd, gm, win)


def _retention_tables(seq):
    d = RET_HEAD_DIM
    pos = jnp.arange(seq, dtype=F32)
    inv_freq = ROPE_BASE ** (-jnp.arange(0, d, 2, dtype=F32) / d)
    ang = jnp.repeat(pos[:, None] * inv_freq[None, :], 2, axis=-1)
    cos = jnp.cos(ang)
    sin = jnp.sin(ang)
    even = (jnp.arange(d) % 2) == 0
    sin_signed = jnp.where(even[None, :], -sin, sin)

    c = RET_CHUNK
    log_g = jnp.log(1.0 - 2.0 ** (-5.0 - jnp.arange(RET_HEADS, dtype=F32)))
    idx = jnp.arange(c, dtype=F32)
    rel = idx[:, None] - idx[None, :]
    decay_in = jnp.where(rel >= 0, jnp.exp(log_g[:, None, None] * jnp.maximum(rel, 0.0)), 0.0)
    zeta = jnp.exp(log_g[:, None] * (c - 1 - idx)[None, :])
    xi = jnp.exp(log_g[:, None] * (idx + 1)[None, :])
    chunk_decay = jnp.exp(log_g * c)
    bcast = lambda v: jnp.broadcast_to(v[:, :, None], (RET_HEADS, c, d))
    zeta_xi = jnp.stack([bcast(zeta), bcast(xi)], axis=1)
    gamma = jnp.broadcast_to(chunk_decay[:, None, None], (RET_HEADS, 8, d))
    return cos, sin_signed, decay_in, zeta_xi, gamma


def _retention_kernel(q_ref, k_ref, v_ref, g_ref, cos_ref, sin_ref, dec_ref, zx_ref, gamma_ref, gain_ref,
                      o_ref, state_ref, qs_ref, ks_ref, kz_ref, vs_ref):
    c = RET_CHUNK
    d = RET_HEAD_DIM
    n_chunks = q_ref.shape[1] // c

    @pl.when(pl.program_id(1) == 0)
    def _():
        state_ref[...] = jnp.zeros_like(state_ref)

    even_lane = (lax.broadcasted_iota(jnp.int32, (c, d), 1) % 2) == 0
    k_scale = RET_HEAD_DIM ** -0.5

    def rotary(t, cos, sin_signed):
        swapped = jnp.where(even_lane, pltpu.roll(t, d - 1, 1), pltpu.roll(t, 1, 1))
        return t * cos + swapped * sin_signed

    def prepare(j):
        rows = pl.ds(pl.multiple_of(j * c, c), c)
        cos = cos_ref[rows, :]
        sin = sin_ref[rows, :]
        for h in range(RET_HEADS):
            lanes = slice(h * d, (h + 1) * d)
            k = rotary(k_ref[0, rows, lanes], cos, sin) * k_scale
            qs_ref[h] = rotary(q_ref[0, rows, lanes], cos, sin).astype(BF16)
            ks_ref[h] = k.astype(BF16)
            kz_ref[h] = (k * zx_ref[h, 0]).astype(BF16)
            vs_ref[h] = v_ref[0, rows, lanes].astype(BF16)

    def mix(j):
        rows = pl.ds(pl.multiple_of(j * c, c), c)
        for h in range(RET_HEADS):
            lanes = slice(h * d, (h + 1) * d)
            qb = qs_ref[h]
            vb = vs_ref[h]
            scores = _dot_nt(qb, ks_ref[h]) * dec_ref[h]
            state = state_ref[h]
            ret = _dot(scores.astype(BF16), vb) + _dot(qb, state.astype(BF16)) * zx_ref[h, 1]
            state_ref[h] = state * gamma_ref[h, 0:1, :] + _dot_tn(kz_ref[h], vb)
            mu = jnp.mean(ret, axis=-1, keepdims=True)
            cen = ret - mu
            var = jnp.mean(cen * cen, axis=-1, keepdims=True)
            y = cen * lax.rsqrt(var + GN_EPS) * gain_ref[:, lanes]
            gate = g_ref[0, rows, lanes]
            o_ref[0, rows, lanes] = (y * (gate * jax.nn.sigmoid(gate))).astype(o_ref.dtype)

    prepare(0)

    def steady(i, carry):
        mix(i)
        prepare(i + 1)
        return carry

    lax.fori_loop(0, n_chunks - 1, steady, 0)
    mix(n_chunks - 1)


def _retention(u3, cos, sin_signed, decay_in, zeta_xi, gamma, gain):
    b, s, _ = u3.shape
    rows = RET_ROWS
    d = RET_HEAD_DIM
    w = RET_WIDTH
    c = RET_CHUNK
    col = lambda j: pl.BlockSpec((1, rows, w), lambda bi, si: (bi, si, j))
    return pl.pallas_call(
        _retention_kernel,
        grid=(b, s // rows),
        in_specs=[
            col(0), col(1), col(2), col(3),
            pl.BlockSpec((rows, d), lambda bi, si: (si, 0)),
            pl.BlockSpec((rows, d), lambda bi, si: (si, 0)),
            _resident(decay_in.shape),
            _resident(zeta_xi.shape),
            _resident(gamma.shape),
            _resident(gain.shape),
        ],
        out_specs=pl.BlockSpec((1, rows, w), lambda bi, si: (bi, si, 0)),
        out_shape=jax.ShapeDtypeStruct((b, s, w), BF16),
        scratch_shapes=[pltpu.VMEM((RET_HEADS, d, d), F32)] + [pltpu.VMEM((RET_HEADS, c, d), BF16)] * 4,
        compiler_params=pltpu.CompilerParams(
            dimension_semantics=("arbitrary", "arbitrary"), vmem_limit_bytes=VMEM_LIMIT_BYTES),
        name="retention",
    )(u3, u3, u3, u3, cos, sin_signed, decay_in, zeta_xi, gamma, gain)


ATT_BLOCKS = ATT_ROWS // BAND_BLOCK
KIND_ACC, KIND_M, KIND_L = 0, 1, 2
N_KINDS = 3


def _attention_bias():
    blk = BAND_BLOCK
    qi = np.arange(blk)[:, None]
    kj = np.arange(2 * blk)[None, :]
    dist = qi + blk - kj
    band = (dist >= 0) & (dist <= ATT_SPAN)
    masks = np.stack([band, band & (kj >= blk)])
    return jnp.asarray(np.where(masks, 0.0, -np.inf).astype(np.float32))


def _dilated_kernel(q1, k1c, k1p, v1c, v1p, q4, k4c, k4p, v4c, v4p, q16, k16c, k16p, v16c, v16p, bias_ref,
                    o_ref, kb0, kb1, kb2, vb0, vb1, vb2, s_ref, e_ref, res_ref, y_ref):
    rows = ATT_ROWS
    blk = BAND_BLOCK
    hd = ATT_HEAD_DIM
    tile = pl.program_id(2)
    n_br = len(DILATIONS)
    kbufs, vbufs = (kb0, kb1, kb2), (vb0, vb1, vb2)
    d4, d16 = DILATIONS[1], DILATIONS[2]
    log2 = lambda n: n.bit_length() - 1

    ones = jnp.ones((blk, LANES), BF16)
    per_branch = (
        ([k1p.at[0]], [k1c.at[0]], [v1p.at[0]], [v1c.at[0]]),
        ([k4p.at[0, r] for r in range(d4)], [k4c.at[0, r] for r in range(d4)],
         [v4p.at[0, r] for r in range(d4)], [v4c.at[0, r] for r in range(d4)]),
        ([k16p.at[0, r] for r in range(d16)], [k16c.at[0, r] for r in range(d16)],
         [v16p.at[0, r] for r in range(d16)], [v16c.at[0, r] for r in range(d16)]),
    )
    for br, (kps, kcs, vps, vcs) in enumerate(per_branch):
        cur = rows // len(kps)
        for r in range(len(kps)):
            base = r * (blk + cur)
            kbufs[br][base:base + blk, :] = kps[r][...]
            kbufs[br][base + blk:base + blk + cur, :] = kcs[r][...]
            vbufs[br][base:base + blk, 0:LANES] = vps[r][...]
            vbufs[br][base + blk:base + blk + cur, 0:LANES] = vcs[r][...]
        for j in range(vbufs[br].shape[0] // blk):
            vbufs[br][j * blk:(j + 1) * blk, LANES:2 * LANES] = ones

    head0 = lax.broadcasted_iota(jnp.int32, (blk, LANES), 1) < hd

    def residue(br, idx):
        return (0, idx >> log2(ATT_BLOCKS // d4), idx)[br]

    def is_first(br, idx):
        jb = (idx, idx & (ATT_BLOCKS // d4 - 1), 0)[br]
        return ((tile == 0) & (jb == 0)).astype(jnp.int32)

    def key_rows(br, idx):
        return pl.ds(pl.multiple_of((idx + residue(br, idx)) * blk, blk), 2 * blk)

    def load_q(br, idx):
        if br == 0:
            return q1[0, pl.ds(pl.multiple_of(idx * blk, blk), blk), :]
        if br == 1:
            jb = idx & (ATT_BLOCKS // d4 - 1)
            return q4[0, residue(br, idx), pl.ds(pl.multiple_of(jb * blk, blk), blk), :]
        return q16[0, idx]

    def store_result(br, idx, kind, val):
        if br == 0:
            res_ref[kind, pl.ds(pl.multiple_of(idx * blk, blk), blk), :] = val
        elif br == 1:
            jb = idx & (ATT_BLOCKS // d4 - 1)
            res_ref[N_KINDS + kind, pl.ds(residue(br, idx) + d4 * blk * jb, blk, stride=d4), :] = val
        else:
            a = idx & (d4 - 1)
            b = idx >> log2(d4)
            y_ref[kind, pl.ds(a * (rows // d4) + b, blk, stride=d4), :] = val

    def pair(lo, hi):
        return jnp.where(head0, jnp.broadcast_to(lo, (blk, LANES)), jnp.broadcast_to(hi, (blk, LANES)))

    def scores(idx):
        for br in range(n_br):
            q = load_q(br, idx)
            zero = jnp.zeros_like(q)
            q2 = jnp.concatenate([jnp.where(head0, q, zero), jnp.where(head0, zero, q)], axis=0)
            s_ref[br] = _dot_nt(q2, kbufs[br][key_rows(br, idx), :])

    def softmax(idx):
        for br in range(n_br):
            bias = bias_ref[is_first(br, idx)]
            maxima = []
            for h in range(2):
                r = slice(h * blk, (h + 1) * blk)
                s = s_ref[br, r, :] + bias
                m = jnp.max(s, axis=-1, keepdims=True)
                e_ref[br, r, :] = jnp.exp(s - m).astype(BF16)
                maxima.append(m)
            store_result(br, idx, KIND_M, pair(*maxima))

    def values(idx):
        for br in range(n_br):
            pv = _dot(e_ref[br], vbufs[br][key_rows(br, idx), :])
            store_result(br, idx, KIND_ACC, pair(pv[0:blk, 0:LANES], pv[blk:2 * blk, 0:LANES]))
            store_result(br, idx, KIND_L, pair(pv[0:blk, LANES:2 * LANES], pv[blk:2 * blk, LANES:2 * LANES]))

    scores(0)
    softmax(0)
    scores(1)

    def steady(i, carry):
        values(i)
        softmax(i + 1)
        scores(i + 2)
        return carry

    lax.fori_loop(0, ATT_BLOCKS - 2, steady, 0, unroll=2)
    values(ATT_BLOCKS - 2)
    softmax(ATT_BLOCKS - 1)
    values(ATT_BLOCKS - 1)

    sub = rows // d4
    for kind in range(N_KINDS):
        for a in range(d4):
            for c in range(sub // blk):
                res_ref[2 * N_KINDS + kind, pl.ds(a + d4 * blk * c, blk, stride=d4), :] = (
                    y_ref[kind, a * sub + c * blk:a * sub + (c + 1) * blk, :])

    def combine(i, carry):
        r = pl.ds(pl.multiple_of(i * blk, blk), blk)
        ms = [res_ref[br * N_KINDS + KIND_M, r, :] for br in range(n_br)]
        mx = jnp.maximum(jnp.maximum(ms[0], ms[1]), ms[2])
        ws = [jnp.exp(m - mx) for m in ms]
        num = sum(w * res_ref[br * N_KINDS + KIND_ACC, r, :] for br, w in enumerate(ws))
        den = sum(w * res_ref[br * N_KINDS + KIND_L, r, :] for br, w in enumerate(ws))
        o_ref[0, r, :] = (num / den).astype(o_ref.dtype)
        return carry

    lax.fori_loop(0, rows // blk, combine, 0)


def _dilated(a1, a4, a16, bias):
    b, s, _ = a1.shape
    rows = ATT_ROWS
    blk = BAND_BLOCK
    pairs = ATT_WIDTH // LANES
    d4, d16 = DILATIONS[1], DILATIONS[2]
    prev_of = lambda n: (lambda i: jnp.maximum(i * n - 1, 0))

    def specs(dil):
        sub = rows // dil
        per = sub // blk
        if dil == 1:
            cur = lambda off: pl.BlockSpec((1, sub, LANES), lambda bi, pi, ti: (bi, ti, off + pi))
            prv = lambda off: pl.BlockSpec((1, blk, LANES),
                                           lambda bi, pi, ti: (bi, prev_of(per)(ti), off + pi))
        else:
            cur = lambda off: pl.BlockSpec((1, dil, sub, LANES), lambda bi, pi, ti: (bi, 0, ti, off + pi))
            prv = lambda off: pl.BlockSpec((1, dil, blk, LANES),
                                           lambda bi, pi, ti: (bi, 0, prev_of(per)(ti), off + pi))
        return [cur(0), cur(pairs), prv(pairs), cur(2 * pairs), prv(2 * pairs)]

    key_rows = lambda dil: (ATT_BLOCKS + dil) * blk
    return pl.pallas_call(
        _dilated_kernel,
        grid=(b, pairs, s // rows),
        in_specs=specs(1) + specs(d4) + specs(d16) + [_resident(bias.shape)],
        out_specs=pl.BlockSpec((1, rows, LANES), lambda bi, pi, ti: (bi, ti, pi)),
        out_shape=jax.ShapeDtypeStruct((b, s, ATT_WIDTH), BF16),
        scratch_shapes=(
            [pltpu.VMEM((key_rows(dil), LANES), BF16) for dil in DILATIONS]
            + [pltpu.VMEM((key_rows(dil), 2 * LANES), BF16) for dil in DILATIONS]
            + [pltpu.VMEM((len(DILATIONS), 2 * blk, 2 * blk), F32),
               pltpu.VMEM((len(DILATIONS), 2 * blk, 2 * blk), BF16),
               pltpu.VMEM((len(DILATIONS) * N_KINDS, rows, LANES), F32),
               pltpu.VMEM((N_KINDS, rows, LANES), F32)]),
        compiler_params=pltpu.CompilerParams(
            dimension_semantics=("arbitrary", "arbitrary", "arbitrary"), vmem_limit_bytes=VMEM_LIMIT_BYTES),
        name="dilated",
    )(a1, a1, a1, a1, a1, a4, a4, a4, a4, a4, a16, a16, a16, a16, a16, bias)


def _out_ffn_kernel(h_ref, ret_ref, att_ref, wo_ref, g2_ref, wg_ref, wu_ref, wd_ref, gf_ref, y_ref,
                    xn_ref, act_ref, *, final_norm):
    mix = jnp.concatenate([ret_ref[...], att_ref[...]], axis=-1)
    h = h_ref[...] + _dot(mix, wo_ref[...])
    xn_ref[...] = _rms_norm(h, g2_ref[...]).astype(BF16)
    h = h + 0.5 * _swiglu_half_step(xn_ref, wg_ref, wu_ref, wd_ref, act_ref)
    if final_norm:
        h = _rms_norm(h, gf_ref[...])
    y_ref[...] = h


def _out_ffn(h1, ret, att, wo, g2, wg, wu, wd, gf, final_norm):
    t, d = h1.shape
    d_ff = wg.shape[1]
    rows = FFN_ROWS
    return pl.pallas_call(
        functools.partial(_out_ffn_kernel, final_norm=final_norm),
        grid=(t // rows,),
        in_specs=[
            pl.BlockSpec((rows, d), lambda i: (i, 0)),
            pl.BlockSpec((rows, ret.shape[1]), lambda i: (i, 0)),
            pl.BlockSpec((rows, att.shape[1]), lambda i: (i, 0)),
            _resident(wo.shape),
            _resident((1, d)),
            _resident((d, d_ff)),
            _resident((d, d_ff)),
            _resident((d_ff, d)),
            _resident((1, d)),
        ],
        out_specs=pl.BlockSpec((rows, d), lambda i: (i, 0)),
        out_shape=jax.ShapeDtypeStruct((t, d), F32),
        scratch_shapes=[pltpu.VMEM((rows, d), BF16), pltpu.VMEM((rows, d_ff), BF16)],
        compiler_params=pltpu.CompilerParams(
            dimension_semantics=("arbitrary",), vmem_limit_bytes=VMEM_LIMIT_BYTES),
        name="out_ffn",
    )(h1, ret, att, wo, g2, wg, wu, wd, gf)


def kernel(x, norm_ffn1, ffn1_w_gate, ffn1_w_up, ffn1_w_down, norm_mix, w_in, ret_norm_gain,
           w_out, norm_ffn2, ffn2_w_gate, ffn2_w_up, ffn2_w_down, norm_final):
    b, s, d = x.shape
    depth = norm_ffn1.shape[0]
    assert s % ATT_ROWS == 0 and s % RET_ROWS == 0 and (b * s) % FFN_ROWS == 0
    assert w_in.shape[2] == 4 * RET_WIDTH + 3 * ATT_WIDTH
    cos, sin_signed, decay_in, zeta_xi, gamma = _retention_tables(s)
    bias = _attention_bias()
    row = lambda v: v.reshape(1, -1)
    bf = lambda w: w.astype(BF16)

    h = x.reshape(b * s, d)
    for l in range(depth):
        h1, u_ret, a1, a4, a16 = _ffn_in(
            h, row(norm_ffn1[l]), bf(ffn1_w_gate[l]), bf(ffn1_w_up[l]), bf(ffn1_w_down[l]),
            row(norm_mix[l]), bf(w_in[l]), b)
        ret = _retention(u_ret.reshape(b, s, -1), cos, sin_signed, decay_in, zeta_xi, gamma,
                         row(ret_norm_gain[l]))
        att = _dilated(a1.reshape(b, s, -1), a4, a16, bias)
        h = _out_ffn(h1, ret.reshape(b * s, -1), att.reshape(b * s, -1), bf(w_out[l]), row(norm_ffn2[l]),
                     bf(ffn2_w_gate[l]), bf(ffn2_w_up[l]), bf(ffn2_w_down[l]), row(norm_final),
                     final_norm=(l == depth - 1))
    return h.reshape(b, s, d)
```

```python
import functools

import jax
import jax.numpy as jnp
import numpy as np
from jax import lax
from jax.experimental import pallas as pl
from jax.experimental.pallas import tpu as pltpu

RET_HEADS = 4
RET_HEAD_DIM = 128
RET_WIDTH = RET_HEADS * RET_HEAD_DIM
ATT_HEADS = 8
ATT_HEAD_DIM = 64
ATT_WIDTH = ATT_HEADS * ATT_HEAD_DIM
DILATIONS = (1, 4, 16)
ATT_SPAN = 128
BAND_BLOCK = 128
ROPE_BASE = 10000.0
NORM_EPS = 1e-6
GN_EPS = 1e-6

LANES = 128
MXU_DIM = 256
VMEM_LIMIT_BYTES = 56 * 1024 * 1024

FFN_ROWS = 256
FF_CHUNK = MXU_DIM
RET_ROWS = 1024
RET_CHUNK = MXU_DIM
ATT_ROWS = ATT_SPAN * DILATIONS[-1]

F32 = jnp.float32
BF16 = jnp.bfloat16


def _dot(a, b):
    return jnp.dot(a, b, preferred_element_type=F32)


def _dot_nt(a, b):
    return lax.dot_general(a, b, (((1,), (1,)), ((), ())), preferred_element_type=F32)


def _dot_tn(a, b):
    return lax.dot_general(a, b, (((0,), (0,)), ((), ())), preferred_element_type=F32)


def _rms_norm(x, gain):
    return x * lax.rsqrt(jnp.mean(x * x, axis=-1, keepdims=True) + NORM_EPS) * gain


def _swiglu_half_step(xn_ref, wg_ref, wu_ref, wd_ref, act_ref):
    d_ff = wg_ref.shape[1]
    for c in range(d_ff // FF_CHUNK):
        cols = slice(c * FF_CHUNK, (c + 1) * FF_CHUNK)
        g = _dot(xn_ref[...], wg_ref[:, cols])
        u = _dot(xn_ref[...], wu_ref[:, cols])
        act_ref[:, cols] = (g * jax.nn.sigmoid(g) * u).astype(BF16)
    return _dot(act_ref[...], wd_ref[...])


def _resident(shape):
    return pl.BlockSpec(shape, lambda *_: (0,) * len(shape), pipeline_mode=pl.Buffered(1))


def _rotary(t, cos, sin_signed, even_lane):
    d = t.shape[-1]
    swapped = jnp.where(even_lane, pltpu.roll(t, d - 1, 1), pltpu.roll(t, 1, 1))
    return t * cos + swapped * sin_signed


def _ffn_in_kernel(x_ref, g1_ref, wg_ref, wu_ref, wd_ref, gm_ref, win_ref, cos_ref, sin_ref, zeta_ref,
                   h_ref, rq_ref, rk_ref, rkz_ref, rv_ref, rg_ref, a1_ref, a4_ref, a16_ref,
                   xn_ref, act_ref, slab_ref, by4_ref):
    rows = x_ref.shape[0]
    n_ret = 4 * RET_WIDTH
    hd = RET_HEAD_DIM
    x = x_ref[...]
    xn_ref[...] = _rms_norm(x, g1_ref[...]).astype(BF16)
    h = x + 0.5 * _swiglu_half_step(xn_ref, wg_ref, wu_ref, wd_ref, act_ref)
    h_ref[...] = h
    xn_ref[...] = _rms_norm(h, gm_ref[...]).astype(BF16)

    even_lane = (lax.broadcasted_iota(jnp.int32, (rows, hd), 1) % 2) == 0
    cos = cos_ref[...]
    sin = sin_ref[...]
    for c in range(RET_WIDTH // MXU_DIM):
        cols = slice(c * MXU_DIM, (c + 1) * MXU_DIM)
        uq = _dot(xn_ref[...], win_ref[:, cols])
        uk = _dot(xn_ref[...], win_ref[:, RET_WIDTH + c * MXU_DIM:RET_WIDTH + (c + 1) * MXU_DIM])
        for half in range(MXU_DIM // hd):
            lanes = slice(c * MXU_DIM + half * hd, c * MXU_DIM + (half + 1) * hd)
            part = slice(half * hd, (half + 1) * hd)
            rq_ref[:, lanes] = _rotary(uq[:, part], cos, sin, even_lane).astype(BF16)
            k = _rotary(uk[:, part], cos, sin, even_lane) * (RET_HEAD_DIM ** -0.5)
            rk_ref[:, lanes] = k.astype(BF16)
            rkz_ref[:, lanes] = (k * zeta_ref[:, lanes]).astype(BF16)

    for c in range(slab_ref.shape[0] * LANES // MXU_DIM):
        cols = slice(c * MXU_DIM, (c + 1) * MXU_DIM)
        u = _dot(xn_ref[...], win_ref[:, n_ret + c * MXU_DIM:n_ret + (c + 1) * MXU_DIM])
        if (c + 1) * MXU_DIM <= ATT_WIDTH:
            u = u * (ATT_HEAD_DIM ** -0.5)
        a1_ref[:, cols] = u.astype(BF16)
        d4, d16 = DILATIONS[1], DILATIONS[2]
        for half in range(MXU_DIM // LANES):
            lanes = slice(c * MXU_DIM + half * LANES, c * MXU_DIM + (half + 1) * LANES)
            slab = slab_ref.at[c * (MXU_DIM // LANES) + half]
            by4 = by4_ref.at[c * (MXU_DIM // LANES) + half]
            slab[...] = u[:, half * LANES:(half + 1) * LANES]
            for a in range(d4):
                by4[a] = slab[pl.ds(a, rows // d4, stride=d4), :]
                a4_ref[0, a, :, lanes] = by4[a].astype(BF16)
            for res in range(d16):
                a16_ref[0, res, :, lanes] = (
                    by4[res % d4, pl.ds(res // d4, rows // d16, stride=d16 // d4), :].astype(BF16))
    rv_ref[...] = _dot(xn_ref[...], win_ref[:, 2 * RET_WIDTH:3 * RET_WIDTH]).astype(BF16)
    rg_ref[...] = _dot(xn_ref[...], win_ref[:, 3 * RET_WIDTH:n_ret])


def _ffn_in(x2, g1, wg, wu, wd, gm, win, cos, sin_signed, zeta_rows, batch):
    t, d = x2.shape
    d_ff = wg.shape[1]
    n_ret = 4 * RET_WIDTH
    n_att = 3 * ATT_WIDTH
    rows = FFN_ROWS
    seq = t // batch
    per_seq = seq // rows
    d4, d16 = DILATIONS[1], DILATIONS[2]
    regroup = lambda dil: pl.BlockSpec((1, dil, rows // dil, n_att),
                                       lambda i: (i // per_seq, 0, i % per_seq, 0))
    by_rows = lambda width: pl.BlockSpec((rows, width), lambda i: (i, 0))
    position = pl.BlockSpec((rows, RET_HEAD_DIM), lambda i: (i % per_seq, 0))
    return pl.pallas_call(
        _ffn_in_kernel,
        grid=(t // rows,),
        in_specs=[
            by_rows(d),
            _resident((1, d)),
            _resident((d, d_ff)),
            _resident((d, d_ff)),
            _resident((d_ff, d)),
            _resident((1, d)),
            _resident((d, n_ret + n_att)),
            position,
            position,
            _resident(zeta_rows.shape),
        ],
        out_specs=[by_rows(d)] + [by_rows(RET_WIDTH)] * 5 + [by_rows(n_att), regroup(d4), regroup(d16)],
        out_shape=(
            [jax.ShapeDtypeStruct((t, d), F32)]
            + [jax.ShapeDtypeStruct((t, RET_WIDTH), BF16)] * 4
            + [jax.ShapeDtypeStruct((t, RET_WIDTH), F32),
               jax.ShapeDtypeStruct((t, n_att), BF16),
               jax.ShapeDtypeStruct((batch, d4, seq // d4, n_att), BF16),
               jax.ShapeDtypeStruct((batch, d16, seq // d16, n_att), BF16)]),
        scratch_shapes=[pltpu.VMEM((rows, d), BF16), pltpu.VMEM((rows, d_ff), BF16),
                        pltpu.VMEM((n_att // LANES, rows, LANES), F32),
                        pltpu.VMEM((n_att // LANES, d4, rows // d4, LANES), F32)],
        compiler_params=pltpu.CompilerParams(
            dimension_semantics=("arbitrary",), vmem_limit_bytes=VMEM_LIMIT_BYTES),
        name="ffn_in",
    )(x2, g1, wg, wu, wd, gm, win, cos, sin_signed, zeta_rows)


def _retention_tables(seq, tile_rows):
    d = RET_HEAD_DIM
    pos = jnp.arange(seq, dtype=F32)
    inv_freq = ROPE_BASE ** (-jnp.arange(0, d, 2, dtype=F32) / d)
    ang = jnp.repeat(pos[:, None] * inv_freq[None, :], 2, axis=-1)
    cos = jnp.cos(ang)
    sin = jnp.sin(ang)
    even = (jnp.arange(d) % 2) == 0
    sin_signed = jnp.where(even[None, :], -sin, sin)

    c = RET_CHUNK
    log_g = jnp.log(1.0 - 2.0 ** (-5.0 - jnp.arange(RET_HEADS, dtype=F32)))
    idx = jnp.arange(c, dtype=F32)
    rel = idx[:, None] - idx[None, :]
    decay_in = jnp.where(rel >= 0, jnp.exp(log_g[:, None, None] * jnp.maximum(rel, 0.0)), 0.0)
    zeta = jnp.exp(log_g[:, None] * (c - 1 - idx)[None, :])
    xi = jnp.exp(log_g[:, None] * (idx + 1)[None, :])
    chunk_decay = jnp.exp(log_g * c)
    xi_lanes = jnp.broadcast_to(xi[:, :, None], (RET_HEADS, c, d))
    gamma = jnp.broadcast_to(chunk_decay[:, None, None], (RET_HEADS, 8, d))
    zeta_rows = jnp.tile(jnp.repeat(zeta.T, d, axis=1), (tile_rows // c, 1))
    return cos, sin_signed, zeta_rows, decay_in, xi_lanes, gamma


def _retention_kernel(q_ref, k_ref, kz_ref, v_ref, dec_ref, xi_ref, gamma_ref, o_ref, state_ref):
    c = RET_CHUNK
    d = RET_HEAD_DIM

    @pl.when(pl.program_id(1) == 0)
    def _():
        state_ref[...] = jnp.zeros_like(state_ref)

    def chunk(j, carry):
        rows = pl.ds(pl.multiple_of(j * c, c), c)
        for h in range(RET_HEADS):
            lanes = slice(h * d, (h + 1) * d)
            qb = q_ref[0, rows, lanes]
            vb = v_ref[0, rows, lanes]
            scores = _dot_nt(qb, k_ref[0, rows, lanes]) * dec_ref[h]
            state = state_ref[h]
            o_ref[0, rows, lanes] = _dot(scores.astype(BF16), vb) + _dot(qb, state.astype(BF16)) * xi_ref[h]
            state_ref[h] = state * gamma_ref[h, 0:1, :] + _dot_tn(kz_ref[0, rows, lanes], vb)
        return carry

    lax.fori_loop(0, q_ref.shape[1] // c, chunk, 0)


def _retention(rq, rk, rkz, rv, decay_in, xi_lanes, gamma):
    b, s, w = rq.shape
    rows = RET_ROWS
    d = RET_HEAD_DIM
    tile = pl.BlockSpec((1, rows, w), lambda bi, si: (bi, si, 0))
    return pl.pallas_call(
        _retention_kernel,
        grid=(b, s // rows),
        in_specs=[tile, tile, tile, tile,
                  _resident(decay_in.shape), _resident(xi_lanes.shape), _resident(gamma.shape)],
        out_specs=tile,
        out_shape=jax.ShapeDtypeStruct((b, s, w), F32),
        scratch_shapes=[pltpu.VMEM((RET_HEADS, d, d), F32)],
        compiler_params=pltpu.CompilerParams(
            dimension_semantics=("arbitrary", "arbitrary"), vmem_limit_bytes=VMEM_LIMIT_BYTES),
        name="retention",
    )(rq, rk, rkz, rv, decay_in, xi_lanes, gamma)


ATT_BLOCKS = ATT_ROWS // BAND_BLOCK
KIND_ACC, KIND_M, KIND_L = 0, 1, 2
N_KINDS = 3


def _attention_bias():
    blk = BAND_BLOCK
    qi = np.arange(blk)[:, None]
    kj = np.arange(2 * blk)[None, :]
    dist = qi + blk - kj
    band = (dist >= 0) & (dist <= ATT_SPAN)
    masks = np.stack([band, band & (kj >= blk)])
    return jnp.asarray(np.where(masks, 0.0, -np.inf).astype(np.float32))


def _dilated_kernel(q1, k1c, k1p, v1c, v1p, q4, k4c, k4p, v4c, v4p, q16, k16c, k16p, v16c, v16p, bias_ref,
                    o_ref, kb0, kb1, kb2, vb0, vb1, vb2, s_ref, e_ref, res_ref, y_ref):
    rows = ATT_ROWS
    blk = BAND_BLOCK
    hd = ATT_HEAD_DIM
    tile = pl.program_id(2)
    n_br = len(DILATIONS)
    kbufs, vbufs = (kb0, kb1, kb2), (vb0, vb1, vb2)
    d4, d16 = DILATIONS[1], DILATIONS[2]
    log2 = lambda n: n.bit_length() - 1

    ones = jnp.ones((blk, LANES), BF16)
    per_branch = (
        ([k1p.at[0]], [k1c.at[0]], [v1p.at[0]], [v1c.at[0]]),
        ([k4p.at[0, r] for r in range(d4)], [k4c.at[0, r] for r in range(d4)],
         [v4p.at[0, r] for r in range(d4)], [v4c.at[0, r] for r in range(d4)]),
        ([k16p.at[0, r] for r in range(d16)], [k16c.at[0, r] for r in range(d16)],
         [v16p.at[0, r] for r in range(d16)], [v16c.at[0, r] for r in range(d16)]),
    )
    for br, (kps, kcs, vps, vcs) in enumerate(per_branch):
        cur = rows // len(kps)
        for r in range(len(kps)):
            base = r * (blk + cur)
            kbufs[br][base:base + blk, :] = kps[r][...]
            kbufs[br][base + blk:base + blk + cur, :] = kcs[r][...]
            vbufs[br][base:base + blk, 0:LANES] = vps[r][...]
            vbufs[br][base + blk:base + blk + cur, 0:LANES] = vcs[r][...]
        for j in range(vbufs[br].shape[0] // blk):
            vbufs[br][j * blk:(j + 1) * blk, LANES:2 * LANES] = ones

    head0 = lax.broadcasted_iota(jnp.int32, (blk, LANES), 1) < hd

    def residue(br, idx):
        return (0, idx >> log2(ATT_BLOCKS // d4), idx)[br]

    def is_first(br, idx):
        jb = (idx, idx & (ATT_BLOCKS // d4 - 1), 0)[br]
        return ((tile == 0) & (jb == 0)).astype(jnp.int32)

    def key_rows(br, idx):
        return pl.ds(pl.multiple_of((idx + residue(br, idx)) * blk, blk), 2 * blk)

    def load_q(br, idx):
        if br == 0:
            return q1[0, pl.ds(pl.multiple_of(idx * blk, blk), blk), :]
        if br == 1:
            jb = idx & (ATT_BLOCKS // d4 - 1)
            return q4[0, residue(br, idx), pl.ds(pl.multiple_of(jb * blk, blk), blk), :]
        return q16[0, idx]

    def store_result(br, idx, kind, val):
        if br == 0:
            res_ref[kind, pl.ds(pl.multiple_of(idx * blk, blk), blk), :] = val
        elif br == 1:
            jb = idx & (ATT_BLOCKS // d4 - 1)
            res_ref[N_KINDS + kind, pl.ds(residue(br, idx) + d4 * blk * jb, blk, stride=d4), :] = val
        else:
            a = idx & (d4 - 1)
            b = idx >> log2(d4)
            y_ref[kind, pl.ds(a * (rows // d4) + b, blk, stride=d4), :] = val

    def pair(lo, hi):
        return jnp.where(head0, jnp.broadcast_to(lo, (blk, LANES)), jnp.broadcast_to(hi, (blk, LANES)))

    def scores(idx):
        for br in range(n_br):
            q = load_q(br, idx)
            zero = jnp.zeros_like(q)
            q2 = jnp.concatenate([jnp.where(head0, q, zero), jnp.where(head0, zero, q)], axis=0)
            s_ref[br] = _dot_nt(q2, kbufs[br][key_rows(br, idx), :])

    def softmax(idx):
        for br in range(n_br):
            bias = bias_ref[is_first(br, idx)]
            maxima = []
            for h in range(2):
                r = slice(h * blk, (h + 1) * blk)
                s = s_ref[br, r, :] + bias
                m = jnp.max(s, axis=-1, keepdims=True)
                e_ref[br, r, :] = jnp.exp(s - m).astype(BF16)
                maxima.append(m)
            store_result(br, idx, KIND_M, pair(*maxima))

    def values(idx):
        for br in range(n_br):
            pv = _dot(e_ref[br], vbufs[br][key_rows(br, idx), :])
            store_result(br, idx, KIND_ACC, pair(pv[0:blk, 0:LANES], pv[blk:2 * blk, 0:LANES]))
            store_result(br, idx, KIND_L, pair(pv[0:blk, LANES:2 * LANES], pv[blk:2 * blk, LANES:2 * LANES]))

    scores(0)
    softmax(0)
    scores(1)

    def steady(i, carry):
        values(i)
        softmax(i + 1)
        scores(i + 2)
        return carry

    lax.fori_loop(0, ATT_BLOCKS - 2, steady, 0, unroll=2)
    values(ATT_BLOCKS - 2)
    softmax(ATT_BLOCKS - 1)
    values(ATT_BLOCKS - 1)

    sub = rows // d4
    for kind in range(N_KINDS):
        for a in range(d4):
            for c in range(sub // blk):
                res_ref[2 * N_KINDS + kind, pl.ds(a + d4 * blk * c, blk, stride=d4), :] = (
                    y_ref[kind, a * sub + c * blk:a * sub + (c + 1) * blk, :])

    def combine(i, carry):
        r = pl.ds(pl.multiple_of(i * blk, blk), blk)
        ms = [res_ref[br * N_KINDS + KIND_M, r, :] for br in range(n_br)]
        mx = jnp.maximum(jnp.maximum(ms[0], ms[1]), ms[2])
        ws = [jnp.exp(m - mx) for m in ms]
        num = sum(w * res_ref[br * N_KINDS + KIND_ACC, r, :] for br, w in enumerate(ws))
        den = sum(w * res_ref[br * N_KINDS + KIND_L, r, :] for br, w in enumerate(ws))
        o_ref[0, r, :] = (num / den).astype(o_ref.dtype)
        return carry

    lax.fori_loop(0, rows // blk, combine, 0)


def _dilated(a1, a4, a16, bias):
    b, s, _ = a1.shape
    rows = ATT_ROWS
    blk = BAND_BLOCK
    pairs = ATT_WIDTH // LANES
    d4, d16 = DILATIONS[1], DILATIONS[2]
    prev_of = lambda n: (lambda i: jnp.maximum(i * n - 1, 0))

    def specs(dil):
        sub = rows // dil
        per = sub // blk
        if dil == 1:
            cur = lambda off: pl.BlockSpec((1, sub, LANES), lambda bi, pi, ti: (bi, ti, off + pi))
            prv = lambda off: pl.BlockSpec((1, blk, LANES),
                                           lambda bi, pi, ti: (bi, prev_of(per)(ti), off + pi))
        else:
            cur = lambda off: pl.BlockSpec((1, dil, sub, LANES), lambda bi, pi, ti: (bi, 0, ti, off + pi))
            prv = lambda off: pl.BlockSpec((1, dil, blk, LANES),
                                           lambda bi, pi, ti: (bi, 0, prev_of(per)(ti), off + pi))
        return [cur(0), cur(pairs), prv(pairs), cur(2 * pairs), prv(2 * pairs)]

    key_rows = lambda dil: (ATT_BLOCKS + dil) * blk
    return pl.pallas_call(
        _dilated_kernel,
        grid=(b, pairs, s // rows),
        in_specs=specs(1) + specs(d4) + specs(d16) + [_resident(bias.shape)],
        out_specs=pl.BlockSpec((1, rows, LANES), lambda bi, pi, ti: (bi, ti, pi)),
        out_shape=jax.ShapeDtypeStruct((b, s, ATT_WIDTH), BF16),
        scratch_shapes=(
            [pltpu.VMEM((key_rows(dil), LANES), BF16) for dil in DILATIONS]
            + [pltpu.VMEM((key_rows(dil), 2 * LANES), BF16) for dil in DILATIONS]
            + [pltpu.VMEM((len(DILATIONS), 2 * blk, 2 * blk), F32),
               pltpu.VMEM((len(DILATIONS), 2 * blk, 2 * blk), BF16),
               pltpu.VMEM((len(DILATIONS) * N_KINDS, rows, LANES), F32),
               pltpu.VMEM((N_KINDS, rows, LANES), F32)]),
        compiler_params=pltpu.CompilerParams(
            dimension_semantics=("arbitrary", "arbitrary", "arbitrary"), vmem_limit_bytes=VMEM_LIMIT_BYTES),
        name="dilated",
    )(a1, a1, a1, a1, a1, a4, a4, a4, a4, a4, a16, a16, a16, a16, a16, bias)


def _out_ffn_kernel(h_ref, ret_ref, gate_ref, att_ref, gain_ref, wo_ref, g2_ref, wg_ref, wu_ref, wd_ref, gf_ref,
                    y_ref, xn_ref, act_ref, mix_ref, *, final_norm):
    d = RET_HEAD_DIM
    h = h_ref[...] + _dot(att_ref[...], wo_ref[RET_WIDTH:, :])
    for hd in range(RET_HEADS):
        lanes = slice(hd * d, (hd + 1) * d)
        ret = ret_ref[:, lanes]
        cen = ret - jnp.mean(ret, axis=-1, keepdims=True)
        var = jnp.mean(cen * cen, axis=-1, keepdims=True)
        y = cen * lax.rsqrt(var + GN_EPS) * gain_ref[:, lanes]
        gate = gate_ref[:, lanes]
        mix_ref[:, lanes] = (y * (gate * jax.nn.sigmoid(gate))).astype(BF16)
    h = h + _dot(mix_ref[...], wo_ref[0:RET_WIDTH, :])
    xn_ref[...] = _rms_norm(h, g2_ref[...]).astype(BF16)
    h = h + 0.5 * _swiglu_half_step(xn_ref, wg_ref, wu_ref, wd_ref, act_ref)
    if final_norm:
        h = _rms_norm(h, gf_ref[...])
    y_ref[...] = h


def _out_ffn(h1, ret, gate, att, gain, wo, g2, wg, wu, wd, gf, final_norm):
    t, d = h1.shape
    d_ff = wg.shape[1]
    rows = FFN_ROWS
    by_rows = lambda a: pl.BlockSpec((rows, a.shape[1]), lambda i: (i, 0))
    return pl.pallas_call(
        functools.partial(_out_ffn_kernel, final_norm=final_norm),
        grid=(t // rows,),
        in_specs=[
            by_rows(h1), by_rows(ret), by_rows(gate), by_rows(att),
            _resident(gain.shape),
            _resident(wo.shape),
            _resident((1, d)),
            _resident((d, d_ff)),
            _resident((d, d_ff)),
            _resident((d_ff, d)),
            _resident((1, d)),
        ],
        out_specs=pl.BlockSpec((rows, d), lambda i: (i, 0)),
        out_shape=jax.ShapeDtypeStruct((t, d), F32),
        scratch_shapes=[pltpu.VMEM((rows, d), BF16), pltpu.VMEM((rows, d_ff), BF16),
                        pltpu.VMEM((rows, RET_WIDTH), BF16)],
        compiler_params=pltpu.CompilerParams(
            dimension_semantics=("arbitrary",), vmem_limit_bytes=VMEM_LIMIT_BYTES),
        name="out_ffn",
    )(h1, ret, gate, att, gain, wo, g2, wg, wu, wd, gf)


def kernel(x, norm_ffn1, ffn1_w_gate, ffn1_w_up, ffn1_w_down, norm_mix, w_in, ret_norm_gain,
           w_out, norm_ffn2, ffn2_w_gate, ffn2_w_up, ffn2_w_down, norm_final):
    b, s, d = x.shape
    depth = norm_ffn1.shape[0]
    assert s % ATT_ROWS == 0 and s % RET_ROWS == 0 and (b * s) % FFN_ROWS == 0
    assert w_in.shape[2] == 4 * RET_WIDTH + 3 * ATT_WIDTH
    assert FFN_ROWS % RET_CHUNK == 0 and RET_ROWS % RET_CHUNK == 0
    cos, sin_signed, zeta_rows, decay_in, xi_lanes, gamma = _retention_tables(s, FFN_ROWS)
    bias = _attention_bias()
    row = lambda v: v.reshape(1, -1)
    bf = lambda w: w.astype(BF16)
    seq = lambda a: a.reshape(b, s, -1)

    h = x.reshape(b * s, d)
    for l in range(depth):
        h1, rq, rk, rkz, rv, rg, a1, a4, a16 = _ffn_in(
            h, row(norm_ffn1[l]), bf(ffn1_w_gate[l]), bf(ffn1_w_up[l]), bf(ffn1_w_down[l]),
            row(norm_mix[l]), bf(w_in[l]), cos, sin_signed, zeta_rows, b)
        ret = _retention(seq(rq), seq(rk), seq(rkz), seq(rv), decay_in, xi_lanes, gamma)
        att = _dilated(seq(a1), a4, a16, bias)
        h = _out_ffn(h1, ret.reshape(b * s, -1), rg, att.reshape(b * s, -1), row(ret_norm_gain[l]),
                     bf(w_out[l]), row(norm_ffn2[l]), bf(ffn2_w_gate[l]), bf(ffn2_w_up[l]),
                     bf(ffn2_w_down[l]), row(norm_final), final_norm=(l == depth - 1))
    return h.reshape(b, s, d)
```

```python
import functools

import jax
import jax.numpy as jnp
import numpy as np
from jax import lax
from jax.experimental import pallas as pl
from jax.experimental.pallas import tpu as pltpu

RET_HEADS = 4
RET_HEAD_DIM = 128
RET_WIDTH = RET_HEADS * RET_HEAD_DIM
ATT_HEADS = 8
ATT_HEAD_DIM = 64
ATT_WIDTH = ATT_HEADS * ATT_HEAD_DIM
DILATIONS = (1, 4, 16)
ATT_SPAN = 128
BAND_BLOCK = 128
ROPE_BASE = 10000.0
NORM_EPS = 1e-6
GN_EPS = 1e-6

LANES = 128
MXU_DIM = 256
VMEM_LIMIT_BYTES = 56 * 1024 * 1024

FFN_ROWS = 256
FF_CHUNK = MXU_DIM
RET_ROWS = 1024
RET_CHUNK = MXU_DIM
ATT_ROWS = ATT_SPAN * DILATIONS[-1]
PIPELINE_UNROLL = 14

F32 = jnp.float32
BF16 = jnp.bfloat16


def _dot(a, b):
    return jnp.dot(a, b, preferred_element_type=F32)


def _dot_nt(a, b):
    return lax.dot_general(a, b, (((1,), (1,)), ((), ())), preferred_element_type=F32)


def _dot_tn(a, b):
    return lax.dot_general(a, b, (((0,), (0,)), ((), ())), preferred_element_type=F32)


def _rms_norm(x, gain):
    return x * lax.rsqrt(jnp.mean(x * x, axis=-1, keepdims=True) + NORM_EPS) * gain


def _swiglu_half_step(xn_ref, wg_ref, wu_ref, wd_ref, act_ref):
    d_ff = wg_ref.shape[1]
    for c in range(d_ff // FF_CHUNK):
        cols = slice(c * FF_CHUNK, (c + 1) * FF_CHUNK)
        g = _dot(xn_ref[...], wg_ref[:, cols])
        u = _dot(xn_ref[...], wu_ref[:, cols])
        act_ref[:, cols] = (g * jax.nn.sigmoid(g) * u).astype(BF16)
    return _dot(act_ref[...], wd_ref[...])


def _resident(shape):
    return pl.BlockSpec(shape, lambda *_: (0,) * len(shape), pipeline_mode=pl.Buffered(1))


def _rotary(t, cos, sin_signed, even_lane):
    d = t.shape[-1]
    swapped = jnp.where(even_lane, pltpu.roll(t, d - 1, 1), pltpu.roll(t, 1, 1))
    return t * cos + swapped * sin_signed


def _ffn_in_kernel(x_ref, g1_ref, wg_ref, wu_ref, wd_ref, gm_ref, win_ref, cos_ref, sin_ref, zeta_ref,
                   h_ref, rq_ref, rk_ref, rkz_ref, rv_ref, rg_ref, a1_ref, a4_ref, a16_ref,
                   xn_ref, act_ref, slab_ref, by4_ref):
    rows = x_ref.shape[0]
    n_ret = 4 * RET_WIDTH
    hd = RET_HEAD_DIM
    x = x_ref[...]
    xn_ref[...] = _rms_norm(x, g1_ref[...]).astype(BF16)
    h = x + 0.5 * _swiglu_half_step(xn_ref, wg_ref, wu_ref, wd_ref, act_ref)
    h_ref[...] = h
    xn_ref[...] = _rms_norm(h, gm_ref[...]).astype(BF16)

    even_lane = (lax.broadcasted_iota(jnp.int32, (rows, hd), 1) % 2) == 0
    cos = cos_ref[...]
    sin = sin_ref[...]
    for c in range(RET_WIDTH // MXU_DIM):
        cols = slice(c * MXU_DIM, (c + 1) * MXU_DIM)
        uq = _dot(xn_ref[...], win_ref[:, cols])
        uk = _dot(xn_ref[...], win_ref[:, RET_WIDTH + c * MXU_DIM:RET_WIDTH + (c + 1) * MXU_DIM])
        for half in range(MXU_DIM // hd):
            lanes = slice(c * MXU_DIM + half * hd, c * MXU_DIM + (half + 1) * hd)
            part = slice(half * hd, (half + 1) * hd)
            rq_ref[:, lanes] = _rotary(uq[:, part], cos, sin, even_lane).astype(BF16)
            k = _rotary(uk[:, part], cos, sin, even_lane) * (RET_HEAD_DIM ** -0.5)
            rk_ref[:, lanes] = k.astype(BF16)
            rkz_ref[:, lanes] = (k * zeta_ref[:, lanes]).astype(BF16)

    for c in range(slab_ref.shape[0] * LANES // MXU_DIM):
        cols = slice(c * MXU_DIM, (c + 1) * MXU_DIM)
        u = _dot(xn_ref[...], win_ref[:, n_ret + c * MXU_DIM:n_ret + (c + 1) * MXU_DIM])
        if (c + 1) * MXU_DIM <= ATT_WIDTH:
            u = u * (ATT_HEAD_DIM ** -0.5)
        a1_ref[:, cols] = u.astype(BF16)
        d4, d16 = DILATIONS[1], DILATIONS[2]
        for half in range(MXU_DIM // LANES):
            lanes = slice(c * MXU_DIM + half * LANES, c * MXU_DIM + (half + 1) * LANES)
            slab = slab_ref.at[c * (MXU_DIM // LANES) + half]
            by4 = by4_ref.at[c * (MXU_DIM // LANES) + half]
            slab[...] = u[:, half * LANES:(half + 1) * LANES]
            for a in range(d4):
                by4[a] = slab[pl.ds(a, rows // d4, stride=d4), :]
                a4_ref[0, a, :, lanes] = by4[a].astype(BF16)
            for res in range(d16):
                a16_ref[0, res, :, lanes] = (
                    by4[res % d4, pl.ds(res // d4, rows // d16, stride=d16 // d4), :].astype(BF16))
    rv_ref[...] = _dot(xn_ref[...], win_ref[:, 2 * RET_WIDTH:3 * RET_WIDTH]).astype(BF16)
    rg_ref[...] = _dot(xn_ref[...], win_ref[:, 3 * RET_WIDTH:n_ret])


def _ffn_in(x2, g1, wg, wu, wd, gm, win, cos, sin_signed, zeta_rows, batch):
    t, d = x2.shape
    d_ff = wg.shape[1]
    n_ret = 4 * RET_WIDTH
    n_att = 3 * ATT_WIDTH
    rows = FFN_ROWS
    seq = t // batch
    per_seq = seq // rows
    d4, d16 = DILATIONS[1], DILATIONS[2]
    regroup = lambda dil: pl.BlockSpec((1, dil, rows // dil, n_att),
                                       lambda i: (i // per_seq, 0, i % per_seq, 0))
    by_rows = lambda width: pl.BlockSpec((rows, width), lambda i: (i, 0))
    position = pl.BlockSpec((rows, RET_HEAD_DIM), lambda i: (i % per_seq, 0))
    return pl.pallas_call(
        _ffn_in_kernel,
        grid=(t // rows,),
        in_specs=[
            by_rows(d),
            _resident((1, d)),
            _resident((d, d_ff)),
            _resident((d, d_ff)),
            _resident((d_ff, d)),
            _resident((1, d)),
            _resident((d, n_ret + n_att)),
            position,
            position,
            _resident(zeta_rows.shape),
        ],
        out_specs=[by_rows(d)] + [by_rows(RET_WIDTH)] * 5 + [by_rows(n_att), regroup(d4), regroup(d16)],
        out_shape=(
            [jax.ShapeDtypeStruct((t, d), F32)]
            + [jax.ShapeDtypeStruct((t, RET_WIDTH), BF16)] * 4
            + [jax.ShapeDtypeStruct((t, RET_WIDTH), F32),
               jax.ShapeDtypeStruct((t, n_att), BF16),
               jax.ShapeDtypeStruct((batch, d4, seq // d4, n_att), BF16),
               jax.ShapeDtypeStruct((batch, d16, seq // d16, n_att), BF16)]),
        scratch_shapes=[pltpu.VMEM((rows, d), BF16), pltpu.VMEM((rows, d_ff), BF16),
                        pltpu.VMEM((n_att // LANES, rows, LANES), F32),
                        pltpu.VMEM((n_att // LANES, d4, rows // d4, LANES), F32)],
        compiler_params=pltpu.CompilerParams(
            dimension_semantics=("arbitrary",), vmem_limit_bytes=VMEM_LIMIT_BYTES),
        name="ffn_in",
    )(x2, g1, wg, wu, wd, gm, win, cos, sin_signed, zeta_rows)


def _retention_tables(seq, tile_rows):
    d = RET_HEAD_DIM
    pos = jnp.arange(seq, dtype=F32)
    inv_freq = ROPE_BASE ** (-jnp.arange(0, d, 2, dtype=F32) / d)
    ang = jnp.repeat(pos[:, None] * inv_freq[None, :], 2, axis=-1)
    cos = jnp.cos(ang)
    sin = jnp.sin(ang)
    even = (jnp.arange(d) % 2) == 0
    sin_signed = jnp.where(even[None, :], -sin, sin)

    c = RET_CHUNK
    log_g = jnp.log(1.0 - 2.0 ** (-5.0 - jnp.arange(RET_HEADS, dtype=F32)))
    idx = jnp.arange(c, dtype=F32)
    rel = idx[:, None] - idx[None, :]
    decay_in = jnp.where(rel >= 0, jnp.exp(log_g[:, None, None] * jnp.maximum(rel, 0.0)), 0.0)
    zeta = jnp.exp(log_g[:, None] * (c - 1 - idx)[None, :])
    xi = jnp.exp(log_g[:, None] * (idx + 1)[None, :])
    chunk_decay = jnp.exp(log_g * c)
    xi_lanes = jnp.broadcast_to(xi[:, :, None], (RET_HEADS, c, d))
    gamma = jnp.broadcast_to(chunk_decay[:, None, None], (RET_HEADS, 8, d))
    zeta_rows = jnp.tile(jnp.repeat(zeta.T, d, axis=1), (tile_rows // c, 1))
    return cos, sin_signed, zeta_rows, decay_in, xi_lanes, gamma


def _retention_kernel(q_ref, k_ref, kz_ref, v_ref, dec_ref, xi_ref, gamma_ref, o_ref, state_ref):
    c = RET_CHUNK
    d = RET_HEAD_DIM

    @pl.when(pl.program_id(1) == 0)
    def _():
        state_ref[...] = jnp.zeros_like(state_ref)

    def chunk(j, carry):
        rows = pl.ds(pl.multiple_of(j * c, c), c)
        for h in range(RET_HEADS):
            lanes = slice(h * d, (h + 1) * d)
            qb = q_ref[0, rows, lanes]
            vb = v_ref[0, rows, lanes]
            scores = _dot_nt(qb, k_ref[0, rows, lanes]) * dec_ref[h]
            state = state_ref[h]
            o_ref[0, rows, lanes] = _dot(scores.astype(BF16), vb) + _dot(qb, state.astype(BF16)) * xi_ref[h]
            state_ref[h] = state * gamma_ref[h, 0:1, :] + _dot_tn(kz_ref[0, rows, lanes], vb)
        return carry

    lax.fori_loop(0, q_ref.shape[1] // c, chunk, 0)


def _retention(rq, rk, rkz, rv, decay_in, xi_lanes, gamma):
    b, s, w = rq.shape
    rows = RET_ROWS
    d = RET_HEAD_DIM
    tile = pl.BlockSpec((1, rows, w), lambda bi, si: (bi, si, 0))
    return pl.pallas_call(
        _retention_kernel,
        grid=(b, s // rows),
        in_specs=[tile, tile, tile, tile,
                  _resident(decay_in.shape), _resident(xi_lanes.shape), _resident(gamma.shape)],
        out_specs=tile,
        out_shape=jax.ShapeDtypeStruct((b, s, w), F32),
        scratch_shapes=[pltpu.VMEM((RET_HEADS, d, d), F32)],
        compiler_params=pltpu.CompilerParams(
            dimension_semantics=("arbitrary", "arbitrary"), vmem_limit_bytes=VMEM_LIMIT_BYTES),
        name="retention",
    )(rq, rk, rkz, rv, decay_in, xi_lanes, gamma)


ATT_BLOCKS = ATT_ROWS // BAND_BLOCK
KIND_ACC, KIND_M, KIND_L = 0, 1, 2
N_KINDS = 3


def _attention_bias():
    blk = BAND_BLOCK
    qi = np.arange(blk)[:, None]
    kj = np.arange(2 * blk)[None, :]
    dist = qi + blk - kj
    band = (dist >= 0) & (dist <= ATT_SPAN)
    masks = np.stack([band, band & (kj >= blk)])
    return jnp.asarray(np.where(masks, 0.0, -np.inf).astype(np.float32))


def _dilated_kernel(q1, k1c, k1p, v1c, v1p, q4, k4c, k4p, v4c, v4p, q16, k16c, k16p, v16c, v16p, bias_ref,
                    o_ref, kb0, kb1, kb2, vb0, vb1, vb2, s_ref, e_ref, res_ref, y_ref):
    rows = ATT_ROWS
    blk = BAND_BLOCK
    hd = ATT_HEAD_DIM
    tile = pl.program_id(2)
    n_br = len(DILATIONS)
    kbufs, vbufs = (kb0, kb1, kb2), (vb0, vb1, vb2)
    d4, d16 = DILATIONS[1], DILATIONS[2]
    log2 = lambda n: n.bit_length() - 1

    @pl.when((pl.program_id(0) == 0) & (pl.program_id(1) == 0) & (tile == 0))
    def _():
        for vbuf in vbufs:
            vbuf[:, LANES:2 * LANES] = jnp.ones((vbuf.shape[0], LANES), BF16)

    per_branch = (
        ([k1p.at[0]], [k1c.at[0]], [v1p.at[0]], [v1c.at[0]]),
        ([k4p.at[0, r] for r in range(d4)], [k4c.at[0, r] for r in range(d4)],
         [v4p.at[0, r] for r in range(d4)], [v4c.at[0, r] for r in range(d4)]),
        ([k16p.at[0, r] for r in range(d16)], [k16c.at[0, r] for r in range(d16)],
         [v16p.at[0, r] for r in range(d16)], [v16c.at[0, r] for r in range(d16)]),
    )
    for br, (kps, kcs, vps, vcs) in enumerate(per_branch):
        cur = rows // len(kps)
        for r in range(len(kps)):
            base = r * (blk + cur)
            kbufs[br][base:base + blk, :] = kps[r][...]
            kbufs[br][base + blk:base + blk + cur, :] = kcs[r][...]
            vbufs[br][base:base + blk, 0:LANES] = vps[r][...]
            vbufs[br][base + blk:base + blk + cur, 0:LANES] = vcs[r][...]

    head0 = lax.broadcasted_iota(jnp.int32, (blk, LANES), 1) < hd

    def residue(br, idx):
        return (0, idx >> log2(ATT_BLOCKS // d4), idx)[br]

    def is_first(br, idx):
        jb = (idx, idx & (ATT_BLOCKS // d4 - 1), 0)[br]
        return ((tile == 0) & (jb == 0)).astype(jnp.int32)

    def key_rows(br, idx):
        return pl.ds(pl.multiple_of((idx + residue(br, idx)) * blk, blk), 2 * blk)

    def load_q(br, idx):
        if br == 0:
            return q1[0, pl.ds(pl.multiple_of(idx * blk, blk), blk), :]
        if br == 1:
            jb = idx & (ATT_BLOCKS // d4 - 1)
            return q4[0, residue(br, idx), pl.ds(pl.multiple_of(jb * blk, blk), blk), :]
        return q16[0, idx]

    def store_result(br, idx, kind, val):
        if br == 0:
            res_ref[kind, pl.ds(pl.multiple_of(idx * blk, blk), blk), :] = val
        elif br == 1:
            jb = idx & (ATT_BLOCKS // d4 - 1)
            res_ref[N_KINDS + kind, pl.ds(residue(br, idx) + d4 * blk * jb, blk, stride=d4), :] = val
        else:
            a = idx & (d4 - 1)
            b = idx >> log2(d4)
            y_ref[kind, pl.ds(a * (rows // d4) + b, blk, stride=d4), :] = val

    def pair(lo, hi):
        return jnp.where(head0, jnp.broadcast_to(lo, (blk, LANES)), jnp.broadcast_to(hi, (blk, LANES)))

    def scores(idx):
        for br in range(n_br):
            q = load_q(br, idx)
            zero = jnp.zeros_like(q)
            q2 = jnp.concatenate([jnp.where(head0, q, zero), jnp.where(head0, zero, q)], axis=0)
            s_ref[br] = _dot_nt(q2, kbufs[br][key_rows(br, idx), :])

    def softmax(idx):
        for br in range(n_br):
            bias = bias_ref[is_first(br, idx)]
            maxima = []
            for h in range(2):
                r = slice(h * blk, (h + 1) * blk)
                s = s_ref[br, r, :] + bias
                m = jnp.max(s, axis=-1, keepdims=True)
                e_ref[br, r, :] = jnp.exp(s - m).astype(BF16)
                maxima.append(m)
            store_result(br, idx, KIND_M, pair(*maxima))

    def values(idx):
        for br in range(n_br):
            pv = _dot(e_ref[br], vbufs[br][key_rows(br, idx), :])
            store_result(br, idx, KIND_ACC, pair(pv[0:blk, 0:LANES], pv[blk:2 * blk, 0:LANES]))
            store_result(br, idx, KIND_L, pair(pv[0:blk, LANES:2 * LANES], pv[blk:2 * blk, LANES:2 * LANES]))

    scores(0)
    softmax(0)
    scores(1)

    def steady(i, carry):
        values(i)
        softmax(i + 1)
        scores(i + 2)
        return carry

    lax.fori_loop(0, ATT_BLOCKS - 2, steady, 0, unroll=PIPELINE_UNROLL)
    values(ATT_BLOCKS - 2)
    softmax(ATT_BLOCKS - 1)
    values(ATT_BLOCKS - 1)

    sub = rows // d4
    for kind in range(N_KINDS):
        for a in range(d4):
            for c in range(sub // blk):
                res_ref[2 * N_KINDS + kind, pl.ds(a + d4 * blk * c, blk, stride=d4), :] = (
                    y_ref[kind, a * sub + c * blk:a * sub + (c + 1) * blk, :])

    def combine(i, carry):
        r = pl.ds(pl.multiple_of(i * blk, blk), blk)
        ms = [res_ref[br * N_KINDS + KIND_M, r, :] for br in range(n_br)]
        mx = jnp.maximum(jnp.maximum(ms[0], ms[1]), ms[2])
        ws = [jnp.exp(m - mx) for m in ms]
        num = sum(w * res_ref[br * N_KINDS + KIND_ACC, r, :] for br, w in enumerate(ws))
        den = sum(w * res_ref[br * N_KINDS + KIND_L, r, :] for br, w in enumerate(ws))
        o_ref[0, r, :] = (num / den).astype(o_ref.dtype)
        return carry

    lax.fori_loop(0, rows // blk, combine, 0)


def _dilated(a1, a4, a16, bias):
    b, s, _ = a1.shape
    rows = ATT_ROWS
    blk = BAND_BLOCK
    pairs = ATT_WIDTH // LANES
    d4, d16 = DILATIONS[1], DILATIONS[2]
    prev_of = lambda n: (lambda i: jnp.maximum(i * n - 1, 0))

    def specs(dil):
        sub = rows // dil
        per = sub // blk
        if dil == 1:
            cur = lambda off: pl.BlockSpec((1, sub, LANES), lambda bi, pi, ti: (bi, ti, off + pi))
            prv = lambda off: pl.BlockSpec((1, blk, LANES),
                                           lambda bi, pi, ti: (bi, prev_of(per)(ti), off + pi))
        else:
            cur = lambda off: pl.BlockSpec((1, dil, sub, LANES), lambda bi, pi, ti: (bi, 0, ti, off + pi))
            prv = lambda off: pl.BlockSpec((1, dil, blk, LANES),
                                           lambda bi, pi, ti: (bi, 0, prev_of(per)(ti), off + pi))
        return [cur(0), cur(pairs), prv(pairs), cur(2 * pairs), prv(2 * pairs)]

    key_rows = lambda dil: (ATT_BLOCKS + dil) * blk
    return pl.pallas_call(
        _dilated_kernel,
        grid=(b, pairs, s // rows),
        in_specs=specs(1) + specs(d4) + specs(d16) + [_resident(bias.shape)],
        out_specs=pl.BlockSpec((1, rows, LANES), lambda bi, pi, ti: (bi, ti, pi)),
        out_shape=jax.ShapeDtypeStruct((b, s, ATT_WIDTH), BF16),
        scratch_shapes=(
            [pltpu.VMEM((key_rows(dil), LANES), BF16) for dil in DILATIONS]
            + [pltpu.VMEM((key_rows(dil), 2 * LANES), BF16) for dil in DILATIONS]
            + [pltpu.VMEM((len(DILATIONS), 2 * blk, 2 * blk), F32),
               pltpu.VMEM((len(DILATIONS), 2 * blk, 2 * blk), BF16),
               pltpu.VMEM((len(DILATIONS) * N_KINDS, rows, LANES), F32),
               pltpu.VMEM((N_KINDS, rows, LANES), F32)]),
        compiler_params=pltpu.CompilerParams(
            dimension_semantics=("arbitrary", "arbitrary", "arbitrary"), vmem_limit_bytes=VMEM_LIMIT_BYTES),
        name="dilated",
    )(a1, a1, a1, a1, a1, a4, a4, a4, a4, a4, a16, a16, a16, a16, a16, bias)


def _out_ffn_kernel(h_ref, ret_ref, gate_ref, att_ref, gain_ref, wo_ref, g2_ref, wg_ref, wu_ref, wd_ref, gf_ref,
                    y_ref, xn_ref, act_ref, mix_ref, *, final_norm):
    d = RET_HEAD_DIM
    h = h_ref[...] + _dot(att_ref[...], wo_ref[RET_WIDTH:, :])
    for hd in range(RET_HEADS):
        lanes = slice(hd * d, (hd + 1) * d)
        ret = ret_ref[:, lanes]
        cen = ret - jnp.mean(ret, axis=-1, keepdims=True)
        var = jnp.mean(cen * cen, axis=-1, keepdims=True)
        y = cen * lax.rsqrt(var + GN_EPS) * gain_ref[:, lanes]
        gate = gate_ref[:, lanes]
        mix_ref[:, lanes] = (y * (gate * jax.nn.sigmoid(gate))).astype(BF16)
    h = h + _dot(mix_ref[...], wo_ref[0:RET_WIDTH, :])
    xn_ref[...] = _rms_norm(h, g2_ref[...]).astype(BF16)
    h = h + 0.5 * _swiglu_half_step(xn_ref, wg_ref, wu_ref, wd_ref, act_ref)
    if final_norm:
        h = _rms_norm(h, gf_ref[...])
    y_ref[...] = h


def _out_ffn(h1, ret, gate, att, gain, wo, g2, wg, wu, wd, gf, final_norm):
    t, d = h1.shape
    d_ff = wg.shape[1]
    rows = FFN_ROWS
    by_rows = lambda a: pl.BlockSpec((rows, a.shape[1]), lambda i: (i, 0))
    return pl.pallas_call(
        functools.partial(_out_ffn_kernel, final_norm=final_norm),
        grid=(t // rows,),
        in_specs=[
            by_rows(h1), by_rows(ret), by_rows(gate), by_rows(att),
            _resident(gain.shape),
            _resident(wo.shape),
            _resident((1, d)),
            _resident((d, d_ff)),
            _resident((d, d_ff)),
            _resident((d_ff, d)),
            _resident((1, d)),
        ],
        out_specs=pl.BlockSpec((rows, d), lambda i: (i, 0)),
        out_shape=jax.ShapeDtypeStruct((t, d), F32),
        scratch_shapes=[pltpu.VMEM((rows, d), BF16), pltpu.VMEM((rows, d_ff), BF16),
                        pltpu.VMEM((rows, RET_WIDTH), BF16)],
        compiler_params=pltpu.CompilerParams(
            dimension_semantics=("arbitrary",), vmem_limit_bytes=VMEM_LIMIT_BYTES),
        name="out_ffn",
    )(h1, ret, gate, att, gain, wo, g2, wg, wu, wd, gf)


def kernel(x, norm_ffn1, ffn1_w_gate, ffn1_w_up, ffn1_w_down, norm_mix, w_in, ret_norm_gain,
           w_out, norm_ffn2, ffn2_w_gate, ffn2_w_up, ffn2_w_down, norm_final):
    b, s, d = x.shape
    depth = norm_ffn1.shape[0]
    assert s % ATT_ROWS == 0 and s % RET_ROWS == 0 and (b * s) % FFN_ROWS == 0
    assert w_in.shape[2] == 4 * RET_WIDTH + 3 * ATT_WIDTH
    assert FFN_ROWS % RET_CHUNK == 0 and RET_ROWS % RET_CHUNK == 0
    cos, sin_signed, zeta_rows, decay_in, xi_lanes, gamma = _retention_tables(s, FFN_ROWS)
    bias = _attention_bias()
    row = lambda v: v.reshape(1, -1)
    bf = lambda w: w.astype(BF16)
    seq = lambda a: a.reshape(b, s, -1)

    h = x.reshape(b * s, d)
    for l in range(depth):
        h1, rq, rk, rkz, rv, rg, a1, a4, a16 = _ffn_in(
            h, row(norm_ffn1[l]), bf(ffn1_w_gate[l]), bf(ffn1_w_up[l]), bf(ffn1_w_down[l]),
            row(norm_mix[l]), bf(w_in[l]), cos, sin_signed, zeta_rows, b)
        ret = _retention(seq(rq), seq(rk), seq(rkz), seq(rv), decay_in, xi_lanes, gamma)
        att = _dilated(seq(a1), a4, a16, bias)
        h = _out_ffn(h1, ret.reshape(b * s, -1), rg, att.reshape(b * s, -1), row(ret_norm_gain[l]),
                     bf(w_out[l]), row(norm_ffn2[l]), bf(ffn2_w_gate[l]), bf(ffn2_w_up[l]),
                     bf(ffn2_w_down[l]), row(norm_final), final_norm=(l == depth - 1))
    return h.reshape(b, s, d)
```

```python
import functools

import jax
import jax.numpy as jnp
import numpy as np
from jax import lax
from jax.experimental import pallas as pl
from jax.experimental.pallas import tpu as pltpu

RET_HEADS = 4
RET_HEAD_DIM = 128
RET_WIDTH = RET_HEADS * RET_HEAD_DIM
ATT_HEADS = 8
ATT_HEAD_DIM = 64
ATT_WIDTH = ATT_HEADS * ATT_HEAD_DIM
DILATIONS = (1, 4, 16)
ATT_SPAN = 128
BAND_BLOCK = 128
ROPE_BASE = 10000.0
NORM_EPS = 1e-6
GN_EPS = 1e-6

LANES = 128
MXU_DIM = 256
VMEM_LIMIT_BYTES = 56 * 1024 * 1024

FFN_ROWS = 256
FF_CHUNK = MXU_DIM
RET_ROWS = 2048
RET_CHUNK = MXU_DIM
ATT_ROWS = ATT_SPAN * DILATIONS[-1]
PIPELINE_UNROLL = 14

F32 = jnp.float32
BF16 = jnp.bfloat16


def _dot(a, b):
    return jnp.dot(a, b, preferred_element_type=F32)


def _dot_nt(a, b):
    return lax.dot_general(a, b, (((1,), (1,)), ((), ())), preferred_element_type=F32)


def _dot_tn(a, b):
    return lax.dot_general(a, b, (((0,), (0,)), ((), ())), preferred_element_type=F32)


def _rms_norm(x, gain):
    return x * lax.rsqrt(jnp.mean(x * x, axis=-1, keepdims=True) + NORM_EPS) * gain


def _swiglu_half_step(xn_ref, wg_ref, wu_ref, wd_ref, act_ref, rows=slice(None)):
    d_ff = wg_ref.shape[1]
    for c in range(d_ff // FF_CHUNK):
        cols = slice(c * FF_CHUNK, (c + 1) * FF_CHUNK)
        g = _dot(xn_ref[rows, :], wg_ref[:, cols])
        u = _dot(xn_ref[rows, :], wu_ref[:, cols])
        act_ref[rows, cols] = (g * jax.nn.sigmoid(g) * u).astype(BF16)
    return _dot(act_ref[rows, :], wd_ref[...])


def _resident(shape):
    return pl.BlockSpec(shape, lambda *_: (0,) * len(shape), pipeline_mode=pl.Buffered(1))


def _rotary(t, cos, sin_signed, even_lane):
    d = t.shape[-1]
    swapped = jnp.where(even_lane, pltpu.roll(t, d - 1, 1), pltpu.roll(t, 1, 1))
    return t * cos + swapped * sin_signed


def _ffn_in_kernel(x_ref, g1_ref, wg_ref, wu_ref, wd_ref, gm_ref, win_ref, cos_ref, sin_ref, zeta_ref,
                   h_ref, rq_ref, rk_ref, rkz_ref, rv_ref, rg_ref, a1_ref, a4_ref, a16_ref,
                   xn_ref, act_ref, slab_ref, by4_ref, hn_ref):
    step = pl.program_id(0)

    @pl.when(step == 0)
    def _():
        hn_ref[1] = jnp.zeros(hn_ref.shape[1:], hn_ref.dtype)

    def body(slot):
        prev = hn_ref.at[1 - slot]
        _project_attention(prev, win_ref, a1_ref, slab_ref)
        _regroup_attention(slab_ref, by4_ref, a4_ref, a16_ref)
        x = x_ref[...]
        xn_ref[...] = _rms_norm(x, g1_ref[...]).astype(BF16)
        h = x + 0.5 * _swiglu_half_step(xn_ref, wg_ref, wu_ref, wd_ref, act_ref)
        h_ref[...] = h
        hn_ref[slot] = _rms_norm(h, gm_ref[...]).astype(BF16)
        _project_retention_qk(prev, win_ref, cos_ref, sin_ref, zeta_ref, rq_ref, rk_ref, rkz_ref)
        _project_retention_vg(prev, win_ref, rv_ref, rg_ref)

    for slot in range(2):
        pl.when(jnp.bitwise_and(step, 1) == slot)(functools.partial(body, slot))


def _project_retention_qk(xn_ref, win_ref, cos_ref, sin_ref, zeta_ref, rq_ref, rk_ref, rkz_ref):
    rows = xn_ref.shape[0]
    hd = RET_HEAD_DIM
    even_lane = (lax.broadcasted_iota(jnp.int32, (rows, hd), 1) % 2) == 0
    cos = cos_ref[...]
    sin = sin_ref[...]
    for c in range(RET_WIDTH // MXU_DIM):
        cols = slice(c * MXU_DIM, (c + 1) * MXU_DIM)
        uq = _dot(xn_ref[...], win_ref[:, cols])
        uk = _dot(xn_ref[...], win_ref[:, RET_WIDTH + c * MXU_DIM:RET_WIDTH + (c + 1) * MXU_DIM])
        for half in range(MXU_DIM // hd):
            lanes = slice(c * MXU_DIM + half * hd, c * MXU_DIM + (half + 1) * hd)
            part = slice(half * hd, (half + 1) * hd)
            rq_ref[:, lanes] = _rotary(uq[:, part], cos, sin, even_lane).astype(BF16)
            k = _rotary(uk[:, part], cos, sin, even_lane) * (RET_HEAD_DIM ** -0.5)
            rk_ref[:, lanes] = k.astype(BF16)
            rkz_ref[:, lanes] = (k * zeta_ref[:, lanes]).astype(BF16)


def _project_attention(xn_ref, win_ref, a1_ref, slab_ref):
    n_ret = 4 * RET_WIDTH
    for c in range(slab_ref.shape[0] * LANES // MXU_DIM):
        cols = slice(c * MXU_DIM, (c + 1) * MXU_DIM)
        u = _dot(xn_ref[...], win_ref[:, n_ret + c * MXU_DIM:n_ret + (c + 1) * MXU_DIM])
        if (c + 1) * MXU_DIM <= ATT_WIDTH:
            u = u * (ATT_HEAD_DIM ** -0.5)
        a1_ref[:, cols] = u.astype(BF16)
        for half in range(MXU_DIM // LANES):
            slab_ref[c * (MXU_DIM // LANES) + half] = u[:, half * LANES:(half + 1) * LANES]


def _regroup_attention(slab_ref, by4_ref, a4_ref, a16_ref):
    rows = slab_ref.shape[1]
    d4, d16 = DILATIONS[1], DILATIONS[2]
    for s in range(slab_ref.shape[0]):
        lanes = slice(s * LANES, (s + 1) * LANES)
        slab = slab_ref.at[s]
        by4 = by4_ref.at[s]
        for a in range(d4):
            by4[a] = slab[pl.ds(a, rows // d4, stride=d4), :]
            a4_ref[0, a, :, lanes] = by4[a].astype(BF16)
        for res in range(d16):
            a16_ref[0, res, :, lanes] = (
                by4[res % d4, pl.ds(res // d4, rows // d16, stride=d16 // d4), :].astype(BF16))


def _project_retention_vg(xn_ref, win_ref, rv_ref, rg_ref):
    rv_ref[...] = _dot(xn_ref[...], win_ref[:, 2 * RET_WIDTH:3 * RET_WIDTH]).astype(BF16)
    rg_ref[...] = _dot(xn_ref[...], win_ref[:, 3 * RET_WIDTH:4 * RET_WIDTH])


def _ffn_in(x2, g1, wg, wu, wd, gm, win, cos, sin_signed, zeta_rows, batch):
    t, d = x2.shape
    d_ff = wg.shape[1]
    n_ret = 4 * RET_WIDTH
    n_att = 3 * ATT_WIDTH
    rows = FFN_ROWS
    seq = t // batch
    per_seq = seq // rows
    d4, d16 = DILATIONS[1], DILATIONS[2]
    n_tiles = t // rows
    ffn_tile = lambda i: jnp.minimum(i, n_tiles - 1)
    proj_tile = lambda i: jnp.maximum(i - 1, 0)
    regroup = lambda dil: pl.BlockSpec((1, dil, rows // dil, n_att),
                                       lambda i: (proj_tile(i) // per_seq, 0, proj_tile(i) % per_seq, 0))
    by_rows = lambda width: pl.BlockSpec((rows, width), lambda i: (proj_tile(i), 0))
    position = pl.BlockSpec((rows, RET_HEAD_DIM), lambda i: (proj_tile(i) % per_seq, 0))
    return pl.pallas_call(
        _ffn_in_kernel,
        grid=(n_tiles + 1,),
        in_specs=[
            pl.BlockSpec((rows, d), lambda i: (ffn_tile(i), 0)),
            _resident((1, d)),
            _resident((d, d_ff)),
            _resident((d, d_ff)),
            _resident((d_ff, d)),
            _resident((1, d)),
            _resident((d, n_ret + n_att)),
            position,
            position,
            _resident(zeta_rows.shape),
        ],
        out_specs=([pl.BlockSpec((rows, d), lambda i: (ffn_tile(i), 0))] + [by_rows(RET_WIDTH)] * 5
                   + [by_rows(n_att), regroup(d4), regroup(d16)]),
        out_shape=(
            [jax.ShapeDtypeStruct((t, d), F32)]
            + [jax.ShapeDtypeStruct((t, RET_WIDTH), BF16)] * 4
            + [jax.ShapeDtypeStruct((t, RET_WIDTH), F32),
               jax.ShapeDtypeStruct((t, n_att), BF16),
               jax.ShapeDtypeStruct((batch, d4, seq // d4, n_att), BF16),
               jax.ShapeDtypeStruct((batch, d16, seq // d16, n_att), BF16)]),
        scratch_shapes=[pltpu.VMEM((rows, d), BF16), pltpu.VMEM((rows, d_ff), BF16),
                        pltpu.VMEM((n_att // LANES, rows, LANES), F32),
                        pltpu.VMEM((n_att // LANES, d4, rows // d4, LANES), F32),
                        pltpu.VMEM((2, rows, d), BF16)],
        compiler_params=pltpu.CompilerParams(
            dimension_semantics=("arbitrary",), vmem_limit_bytes=VMEM_LIMIT_BYTES),
        name="ffn_in",
    )(x2, g1, wg, wu, wd, gm, win, cos, sin_signed, zeta_rows)


def _retention_tables(seq, tile_rows):
    d = RET_HEAD_DIM
    pos = jnp.arange(seq, dtype=F32)
    inv_freq = ROPE_BASE ** (-jnp.arange(0, d, 2, dtype=F32) / d)
    ang = jnp.repeat(pos[:, None] * inv_freq[None, :], 2, axis=-1)
    cos = jnp.cos(ang)
    sin = jnp.sin(ang)
    even = (jnp.arange(d) % 2) == 0
    sin_signed = jnp.where(even[None, :], -sin, sin)

    c = RET_CHUNK
    log_g = jnp.log(1.0 - 2.0 ** (-5.0 - jnp.arange(RET_HEADS, dtype=F32)))
    idx = jnp.arange(c, dtype=F32)
    rel = idx[:, None] - idx[None, :]
    decay_in = jnp.where(rel >= 0, jnp.exp(log_g[:, None, None] * jnp.maximum(rel, 0.0)), 0.0)
    zeta = jnp.exp(log_g[:, None] * (c - 1 - idx)[None, :])
    xi = jnp.exp(log_g[:, None] * (idx + 1)[None, :])
    chunk_decay = jnp.exp(log_g * c)
    xi_lanes = jnp.broadcast_to(xi[:, :, None], (RET_HEADS, c, d))
    gamma = jnp.broadcast_to(chunk_decay[:, None, None], (RET_HEADS, 8, d))
    zeta_rows = jnp.tile(jnp.repeat(zeta.T, d, axis=1), (tile_rows // c, 1))
    return cos, sin_signed, zeta_rows, decay_in, xi_lanes, gamma


def _retention_kernel(q_ref, k_ref, kz_ref, v_ref, dec_ref, xi_ref, gamma_ref, o_ref, state_ref):
    c = RET_CHUNK
    d = RET_HEAD_DIM

    @pl.when(pl.program_id(1) == 0)
    def _():
        state_ref[...] = jnp.zeros_like(state_ref)

    def chunk(j, carry):
        rows = pl.ds(pl.multiple_of(j * c, c), c)
        for h in range(RET_HEADS):
            lanes = slice(h * d, (h + 1) * d)
            qb = q_ref[0, rows, lanes]
            vb = v_ref[0, rows, lanes]
            scores = _dot_nt(qb, k_ref[0, rows, lanes]) * dec_ref[h]
            state = state_ref[h]
            o_ref[0, rows, lanes] = _dot(scores.astype(BF16), vb) + _dot(qb, state.astype(BF16)) * xi_ref[h]
            state_ref[h] = state * gamma_ref[h, 0:1, :] + _dot_tn(kz_ref[0, rows, lanes], vb)
        return carry

    lax.fori_loop(0, q_ref.shape[1] // c, chunk, 0, unroll=True)


def _retention(rq, rk, rkz, rv, decay_in, xi_lanes, gamma):
    b, s, w = rq.shape
    rows = RET_ROWS
    d = RET_HEAD_DIM
    tile = pl.BlockSpec((1, rows, w), lambda bi, si: (bi, si, 0))
    return pl.pallas_call(
        _retention_kernel,
        grid=(b, s // rows),
        in_specs=[tile, tile, tile, tile,
                  _resident(decay_in.shape), _resident(xi_lanes.shape), _resident(gamma.shape)],
        out_specs=tile,
        out_shape=jax.ShapeDtypeStruct((b, s, w), F32),
        scratch_shapes=[pltpu.VMEM((RET_HEADS, d, d), F32)],
        compiler_params=pltpu.CompilerParams(
            dimension_semantics=("arbitrary", "arbitrary"), vmem_limit_bytes=VMEM_LIMIT_BYTES),
        name="retention",
    )(rq, rk, rkz, rv, decay_in, xi_lanes, gamma)


ATT_BLOCKS = ATT_ROWS // BAND_BLOCK
KIND_ACC, KIND_M, KIND_L = 0, 1, 2
N_KINDS = 3


def _attention_bias():
    blk = BAND_BLOCK
    qi = np.arange(blk)[:, None]
    kj = np.arange(2 * blk)[None, :]
    dist = qi + blk - kj
    band = (dist >= 0) & (dist <= ATT_SPAN)
    masks = np.stack([band, band & (kj >= blk)])
    return jnp.asarray(np.where(masks, 0.0, -np.inf).astype(np.float32))


def _dilated_kernel(q1, k1c, k1p, v1c, v1p, q4, k4c, k4p, v4c, v4p, q16, k16c, k16p, v16c, v16p, bias_ref,
                    o_ref, kb0, kb1, kb2, vb0, vb1, vb2, s_ref, e_ref, res_ref, y_ref):
    rows = ATT_ROWS
    blk = BAND_BLOCK
    hd = ATT_HEAD_DIM
    tile = pl.program_id(2)
    n_br = len(DILATIONS)
    kbufs, vbufs = (kb0, kb1, kb2), (vb0, vb1, vb2)
    d4, d16 = DILATIONS[1], DILATIONS[2]
    log2 = lambda n: n.bit_length() - 1

    @pl.when((pl.program_id(0) == 0) & (pl.program_id(1) == 0) & (tile == 0))
    def _():
        for vbuf in vbufs:
            vbuf[:, LANES:2 * LANES] = jnp.ones((vbuf.shape[0], LANES), BF16)

    per_branch = (
        ([k1p.at[0]], [k1c.at[0]], [v1p.at[0]], [v1c.at[0]]),
        ([k4p.at[0, r] for r in range(d4)], [k4c.at[0, r] for r in range(d4)],
         [v4p.at[0, r] for r in range(d4)], [v4c.at[0, r] for r in range(d4)]),
        ([k16p.at[0, r] for r in range(d16)], [k16c.at[0, r] for r in range(d16)],
         [v16p.at[0, r] for r in range(d16)], [v16c.at[0, r] for r in range(d16)]),
    )
    for br, (kps, kcs, vps, vcs) in enumerate(per_branch):
        cur = rows // len(kps)
        for r in range(len(kps)):
            base = r * (blk + cur)
            kbufs[br][base:base + blk, :] = kps[r][...]
            kbufs[br][base + blk:base + blk + cur, :] = kcs[r][...]
            vbufs[br][base:base + blk, 0:LANES] = vps[r][...]
            vbufs[br][base + blk:base + blk + cur, 0:LANES] = vcs[r][...]

    head0 = lax.broadcasted_iota(jnp.int32, (blk, LANES), 1) < hd

    def residue(br, idx):
        return (0, idx >> log2(ATT_BLOCKS // d4), idx)[br]

    def is_first(br, idx):
        jb = (idx, idx & (ATT_BLOCKS // d4 - 1), 0)[br]
        return ((tile == 0) & (jb == 0)).astype(jnp.int32)

    def key_rows(br, idx):
        return pl.ds(pl.multiple_of((idx + residue(br, idx)) * blk, blk), 2 * blk)

    def load_q(br, idx):
        if br == 0:
            return q1[0, pl.ds(pl.multiple_of(idx * blk, blk), blk), :]
        if br == 1:
            jb = idx & (ATT_BLOCKS // d4 - 1)
            return q4[0, residue(br, idx), pl.ds(pl.multiple_of(jb * blk, blk), blk), :]
        return q16[0, idx]

    def store_result(br, idx, kind, val):
        if br == 0:
            res_ref[kind, pl.ds(pl.multiple_of(idx * blk, blk), blk), :] = val
        elif br == 1:
            jb = idx & (ATT_BLOCKS // d4 - 1)
            res_ref[N_KINDS + kind, pl.ds(residue(br, idx) + d4 * blk * jb, blk, stride=d4), :] = val
        else:
            a = idx & (d4 - 1)
            b = idx >> log2(d4)
            y_ref[kind, pl.ds(a * (rows // d4) + b, blk, stride=d4), :] = val

    def pair(lo, hi):
        return jnp.where(head0, jnp.broadcast_to(lo, (blk, LANES)), jnp.broadcast_to(hi, (blk, LANES)))

    def scores(idx):
        for br in range(n_br):
            q = load_q(br, idx)
            zero = jnp.zeros_like(q)
            q2 = jnp.concatenate([jnp.where(head0, q, zero), jnp.where(head0, zero, q)], axis=0)
            s_ref[br] = _dot_nt(q2, kbufs[br][key_rows(br, idx), :])

    def softmax(idx):
        for br in range(n_br):
            bias = bias_ref[is_first(br, idx)]
            maxima = []
            for h in range(2):
                r = slice(h * blk, (h + 1) * blk)
                s = s_ref[br, r, :] + bias
                m = jnp.max(s, axis=-1, keepdims=True)
                e_ref[br, r, :] = jnp.exp(s - m).astype(BF16)
                maxima.append(m)
            store_result(br, idx, KIND_M, pair(*maxima))

    def values(idx):
        for br in range(n_br):
            pv = _dot(e_ref[br], vbufs[br][key_rows(br, idx), :])
            store_result(br, idx, KIND_ACC, pair(pv[0:blk, 0:LANES], pv[blk:2 * blk, 0:LANES]))
            store_result(br, idx, KIND_L, pair(pv[0:blk, LANES:2 * LANES], pv[blk:2 * blk, LANES:2 * LANES]))

    scores(0)
    softmax(0)
    scores(1)

    def steady(i, carry):
        values(i)
        softmax(i + 1)
        scores(i + 2)
        return carry

    lax.fori_loop(0, ATT_BLOCKS - 2, steady, 0, unroll=PIPELINE_UNROLL)
    values(ATT_BLOCKS - 2)
    softmax(ATT_BLOCKS - 1)
    values(ATT_BLOCKS - 1)

    sub = rows // d4
    for kind in range(N_KINDS):
        for a in range(d4):
            for c in range(sub // blk):
                res_ref[2 * N_KINDS + kind, pl.ds(a + d4 * blk * c, blk, stride=d4), :] = (
                    y_ref[kind, a * sub + c * blk:a * sub + (c + 1) * blk, :])

    def combine(i, carry):
        r = pl.ds(pl.multiple_of(i * blk, blk), blk)
        ms = [res_ref[br * N_KINDS + KIND_M, r, :] for br in range(n_br)]
        mx = jnp.maximum(jnp.maximum(ms[0], ms[1]), ms[2])
        ws = [jnp.exp(m - mx) for m in ms]
        num = sum(w * res_ref[br * N_KINDS + KIND_ACC, r, :] for br, w in enumerate(ws))
        den = sum(w * res_ref[br * N_KINDS + KIND_L, r, :] for br, w in enumerate(ws))
        o_ref[0, r, :] = (num / den).astype(o_ref.dtype)
        return carry

    lax.fori_loop(0, rows // blk, combine, 0)


def _dilated(a1, a4, a16, bias):
    b, s, _ = a1.shape
    rows = ATT_ROWS
    blk = BAND_BLOCK
    pairs = ATT_WIDTH // LANES
    d4, d16 = DILATIONS[1], DILATIONS[2]
    prev_of = lambda n: (lambda i: jnp.maximum(i * n - 1, 0))

    def specs(dil):
        sub = rows // dil
        per = sub // blk
        if dil == 1:
            cur = lambda off: pl.BlockSpec((1, sub, LANES), lambda bi, pi, ti: (bi, ti, off + pi))
            prv = lambda off: pl.BlockSpec((1, blk, LANES),
                                           lambda bi, pi, ti: (bi, prev_of(per)(ti), off + pi))
        else:
            cur = lambda off: pl.BlockSpec((1, dil, sub, LANES), lambda bi, pi, ti: (bi, 0, ti, off + pi))
            prv = lambda off: pl.BlockSpec((1, dil, blk, LANES),
                                           lambda bi, pi, ti: (bi, 0, prev_of(per)(ti), off + pi))
        return [cur(0), cur(pairs), prv(pairs), cur(2 * pairs), prv(2 * pairs)]

    key_rows = lambda dil: (ATT_BLOCKS + dil) * blk
    return pl.pallas_call(
        _dilated_kernel,
        grid=(b, pairs, s // rows),
        in_specs=specs(1) + specs(d4) + specs(d16) + [_resident(bias.shape)],
        out_specs=pl.BlockSpec((1, rows, LANES), lambda bi, pi, ti: (bi, ti, pi)),
        out_shape=jax.ShapeDtypeStruct((b, s, ATT_WIDTH), BF16),
        scratch_shapes=(
            [pltpu.VMEM((key_rows(dil), LANES), BF16) for dil in DILATIONS]
            + [pltpu.VMEM((key_rows(dil), 2 * LANES), BF16) for dil in DILATIONS]
            + [pltpu.VMEM((len(DILATIONS), 2 * blk, 2 * blk), F32),
               pltpu.VMEM((len(DILATIONS), 2 * blk, 2 * blk), BF16),
               pltpu.VMEM((len(DILATIONS) * N_KINDS, rows, LANES), F32),
               pltpu.VMEM((N_KINDS, rows, LANES), F32)]),
        compiler_params=pltpu.CompilerParams(
            dimension_semantics=("arbitrary", "arbitrary", "arbitrary"), vmem_limit_bytes=VMEM_LIMIT_BYTES),
        name="dilated",
    )(a1, a1, a1, a1, a1, a4, a4, a4, a4, a4, a16, a16, a16, a16, a16, bias)


def _out_ffn_kernel(h_ref, ret_ref, gate_ref, att_ref, gain_ref, wo_ref, g2_ref, wg_ref, wu_ref, wd_ref, gf_ref,
                    y_ref, xn_ref, act_ref, mix_ref, h2_ref, *, final_norm):
    d = RET_HEAD_DIM
    step = pl.program_id(0)

    @pl.when(step == 0)
    def _():
        h2_ref[1] = jnp.zeros(h2_ref.shape[1:], h2_ref.dtype)

    def body(slot):
        h2_ref[slot] = h_ref[...] + _dot(att_ref[...], wo_ref[RET_WIDTH:, :])
        h_prev = h2_ref[1 - slot]
        xn_ref[...] = _rms_norm(h_prev, g2_ref[...]).astype(BF16)
        for hd in range(RET_HEADS):
            lanes = slice(hd * d, (hd + 1) * d)
            ret = ret_ref[:, lanes]
            cen = ret - jnp.mean(ret, axis=-1, keepdims=True)
            var = jnp.mean(cen * cen, axis=-1, keepdims=True)
            y = cen * lax.rsqrt(var + GN_EPS) * gain_ref[:, lanes]
            gate = gate_ref[:, lanes]
            mix_ref[:, lanes] = (y * (gate * jax.nn.sigmoid(gate))).astype(BF16)
        out = h2_ref[1 - slot] + 0.5 * _swiglu_half_step(xn_ref, wg_ref, wu_ref, wd_ref, act_ref)
        h2_ref[slot] = h2_ref[slot] + _dot(mix_ref[...], wo_ref[0:RET_WIDTH, :])
        if final_norm:
            out = _rms_norm(out, gf_ref[...])
        y_ref[...] = out

    for slot in range(2):
        pl.when(jnp.bitwise_and(step, 1) == slot)(functools.partial(body, slot))


def _out_ffn(h1, ret, gate, att, gain, wo, g2, wg, wu, wd, gf, final_norm):
    t, d = h1.shape
    d_ff = wg.shape[1]
    rows = FFN_ROWS
    n_tiles = t // rows
    by_rows = lambda a: pl.BlockSpec((rows, a.shape[1]), lambda i: (jnp.minimum(i, n_tiles - 1), 0))
    return pl.pallas_call(
        functools.partial(_out_ffn_kernel, final_norm=final_norm),
        grid=(n_tiles + 1,),
        in_specs=[
            by_rows(h1), by_rows(ret), by_rows(gate), by_rows(att),
            _resident(gain.shape),
            _resident(wo.shape),
            _resident((1, d)),
            _resident((d, d_ff)),
            _resident((d, d_ff)),
            _resident((d_ff, d)),
            _resident((1, d)),
        ],
        out_specs=pl.BlockSpec((rows, d), lambda i: (jnp.maximum(i - 1, 0), 0)),
        out_shape=jax.ShapeDtypeStruct((t, d), F32),
        scratch_shapes=[pltpu.VMEM((rows, d), BF16), pltpu.VMEM((rows, d_ff), BF16),
                        pltpu.VMEM((rows, RET_WIDTH), BF16), pltpu.VMEM((2, rows, d), F32)],
        compiler_params=pltpu.CompilerParams(
            dimension_semantics=("arbitrary",), vmem_limit_bytes=VMEM_LIMIT_BYTES),
        name="out_ffn",
    )(h1, ret, gate, att, gain, wo, g2, wg, wu, wd, gf)


def kernel(x, norm_ffn1, ffn1_w_gate, ffn1_w_up, ffn1_w_down, norm_mix, w_in, ret_norm_gain,
           w_out, norm_ffn2, ffn2_w_gate, ffn2_w_up, ffn2_w_down, norm_final):
    b, s, d = x.shape
    depth = norm_ffn1.shape[0]
    assert s % ATT_ROWS == 0 and s % RET_ROWS == 0 and (b * s) % FFN_ROWS == 0
    assert w_in.shape[2] == 4 * RET_WIDTH + 3 * ATT_WIDTH
    assert FFN_ROWS % RET_CHUNK == 0 and RET_ROWS % RET_CHUNK == 0
    cos, sin_signed, zeta_rows, decay_in, xi_lanes, gamma = _retention_tables(s, FFN_ROWS)
    bias = _attention_bias()
    row = lambda v: v.reshape(1, -1)
    bf = lambda w: w.astype(BF16)
    seq = lambda a: a.reshape(b, s, -1)

    h = x.reshape(b * s, d)
    for l in range(depth):
        h1, rq, rk, rkz, rv, rg, a1, a4, a16 = _ffn_in(
            h, row(norm_ffn1[l]), bf(ffn1_w_gate[l]), bf(ffn1_w_up[l]), bf(ffn1_w_down[l]),
            row(norm_mix[l]), bf(w_in[l]), cos, sin_signed, zeta_rows, b)
        ret = _retention(seq(rq), seq(rk), seq(rkz), seq(rv), decay_in, xi_lanes, gamma)
        att = _dilated(seq(a1), a4, a16, bias)
        h = _out_ffn(h1, ret.reshape(b * s, -1), rg, att.reshape(b * s, -1), row(ret_norm_gain[l]),
                     bf(w_out[l]), row(norm_ffn2[l]), bf(ffn2_w_gate[l]), bf(ffn2_w_up[l]),
                     bf(ffn2_w_down[l]), row(norm_final), final_norm=(l == depth - 1))
    return h.reshape(b, s, d)
```

```python
import functools

import jax
import jax.numpy as jnp
import numpy as np
from jax import lax
from jax.experimental import pallas as pl
from jax.experimental.pallas import tpu as pltpu

RET_HEADS = 4
RET_HEAD_DIM = 128
RET_WIDTH = RET_HEADS * RET_HEAD_DIM
ATT_HEADS = 8
ATT_HEAD_DIM = 64
ATT_WIDTH = ATT_HEADS * ATT_HEAD_DIM
DILATIONS = (1, 4, 16)
ATT_SPAN = 128
BAND_BLOCK = 128
ROPE_BASE = 10000.0
NORM_EPS = 1e-6
GN_EPS = 1e-6

LANES = 128
MXU_DIM = 256
VMEM_LIMIT_BYTES = 56 * 1024 * 1024

FFN_ROWS = 256
FF_CHUNK = MXU_DIM
RET_ROWS = 2048
RET_CHUNK = MXU_DIM
ATT_ROWS = ATT_SPAN * DILATIONS[-1]

F32 = jnp.float32
BF16 = jnp.bfloat16


def _dot(a, b):
    return jnp.dot(a, b, preferred_element_type=F32)


def _dot_nt(a, b):
    return lax.dot_general(a, b, (((1,), (1,)), ((), ())), preferred_element_type=F32)


def _dot_tn(a, b):
    return lax.dot_general(a, b, (((0,), (0,)), ((), ())), preferred_element_type=F32)


def _rms_norm(x, gain):
    return x * lax.rsqrt(jnp.mean(x * x, axis=-1, keepdims=True) + NORM_EPS) * gain


def _swiglu_half_step(xn_ref, wg_ref, wu_ref, wd_ref, act_ref, rows=slice(None)):
    d_ff = wg_ref.shape[1]
    for c in range(d_ff // FF_CHUNK):
        cols = slice(c * FF_CHUNK, (c + 1) * FF_CHUNK)
        g = _dot(xn_ref[rows, :], wg_ref[:, cols])
        u = _dot(xn_ref[rows, :], wu_ref[:, cols])
        act_ref[rows, cols] = (g * jax.nn.sigmoid(g) * u).astype(BF16)
    return _dot(act_ref[rows, :], wd_ref[...])


def _resident(shape):
    return pl.BlockSpec(shape, lambda *_: (0,) * len(shape), pipeline_mode=pl.Buffered(1))


def _rotary(t, cos, sin_signed, even_lane):
    d = t.shape[-1]
    swapped = jnp.where(even_lane, pltpu.roll(t, d - 1, 1), pltpu.roll(t, 1, 1))
    return t * cos + swapped * sin_signed


def _ffn_in_kernel(x_ref, g1_ref, wg_ref, wu_ref, wd_ref, gm_ref, win_ref, cos_ref, sin_ref, zeta_ref,
                   h_ref, rq_ref, rk_ref, rkz_ref, rv_ref, rg_ref, a1_ref, a4_ref, a16_ref,
                   xn_ref, act_ref, slab_ref, by4_ref, hn_ref):
    step = pl.program_id(0)

    @pl.when(step == 0)
    def _():
        hn_ref[1] = jnp.zeros(hn_ref.shape[1:], hn_ref.dtype)

    def body(slot):
        prev = hn_ref.at[1 - slot]
        _project_attention(prev, win_ref, a1_ref, slab_ref)
        _regroup_attention(slab_ref, by4_ref, a4_ref, a16_ref)
        x = x_ref[...]
        xn_ref[...] = _rms_norm(x, g1_ref[...]).astype(BF16)
        h = x + 0.5 * _swiglu_half_step(xn_ref, wg_ref, wu_ref, wd_ref, act_ref)
        h_ref[...] = h
        hn_ref[slot] = _rms_norm(h, gm_ref[...]).astype(BF16)
        _project_retention_qk(prev, win_ref, cos_ref, sin_ref, zeta_ref, rq_ref, rk_ref, rkz_ref)
        _project_retention_vg(prev, win_ref, rv_ref, rg_ref)

    for slot in range(2):
        pl.when(jnp.bitwise_and(step, 1) == slot)(functools.partial(body, slot))


def _project_retention_qk(xn_ref, win_ref, cos_ref, sin_ref, zeta_ref, rq_ref, rk_ref, rkz_ref):
    rows = xn_ref.shape[0]
    hd = RET_HEAD_DIM
    even_lane = (lax.broadcasted_iota(jnp.int32, (rows, hd), 1) % 2) == 0
    cos = cos_ref[...]
    sin = sin_ref[...]
    for c in range(RET_WIDTH // MXU_DIM):
        cols = slice(c * MXU_DIM, (c + 1) * MXU_DIM)
        uq = _dot(xn_ref[...], win_ref[:, cols])
        uk = _dot(xn_ref[...], win_ref[:, RET_WIDTH + c * MXU_DIM:RET_WIDTH + (c + 1) * MXU_DIM])
        for half in range(MXU_DIM // hd):
            lanes = slice(c * MXU_DIM + half * hd, c * MXU_DIM + (half + 1) * hd)
            part = slice(half * hd, (half + 1) * hd)
            rq_ref[:, lanes] = _rotary(uq[:, part], cos, sin, even_lane).astype(BF16)
            k = _rotary(uk[:, part], cos, sin, even_lane) * (RET_HEAD_DIM ** -0.5)
            rk_ref[:, lanes] = k.astype(BF16)
            rkz_ref[:, lanes] = (k * zeta_ref[:, lanes]).astype(BF16)


def _project_attention(xn_ref, win_ref, a1_ref, slab_ref):
    n_ret = 4 * RET_WIDTH
    for c in range(slab_ref.shape[0] * LANES // MXU_DIM):
        cols = slice(c * MXU_DIM, (c + 1) * MXU_DIM)
        u = _dot(xn_ref[...], win_ref[:, n_ret + c * MXU_DIM:n_ret + (c + 1) * MXU_DIM])
        if (c + 1) * MXU_DIM <= ATT_WIDTH:
            u = u * (ATT_HEAD_DIM ** -0.5)
        a1_ref[:, cols] = u.astype(BF16)
        for half in range(MXU_DIM // LANES):
            slab_ref[c * (MXU_DIM // LANES) + half] = u[:, half * LANES:(half + 1) * LANES]


def _regroup_attention(slab_ref, by4_ref, a4_ref, a16_ref):
    rows = slab_ref.shape[1]
    d4, d16 = DILATIONS[1], DILATIONS[2]
    for s in range(slab_ref.shape[0]):
        lanes = slice(s * LANES, (s + 1) * LANES)
        slab = slab_ref.at[s]
        by4 = by4_ref.at[s]
        for a in range(d4):
            by4[a] = slab[pl.ds(a, rows // d4, stride=d4), :]
            a4_ref[0, a, :, lanes] = by4[a].astype(BF16)
        for res in range(d16):
            a16_ref[0, res, :, lanes] = (
                by4[res % d4, pl.ds(res // d4, rows // d16, stride=d16 // d4), :].astype(BF16))


def _project_retention_vg(xn_ref, win_ref, rv_ref, rg_ref):
    rv_ref[...] = _dot(xn_ref[...], win_ref[:, 2 * RET_WIDTH:3 * RET_WIDTH]).astype(BF16)
    rg_ref[...] = _dot(xn_ref[...], win_ref[:, 3 * RET_WIDTH:4 * RET_WIDTH])


def _ffn_in(x2, g1, wg, wu, wd, gm, win, cos, sin_signed, zeta_rows, batch):
    t, d = x2.shape
    d_ff = wg.shape[1]
    n_ret = 4 * RET_WIDTH
    n_att = 3 * ATT_WIDTH
    rows = FFN_ROWS
    seq = t // batch
    per_seq = seq // rows
    d4, d16 = DILATIONS[1], DILATIONS[2]
    n_tiles = t // rows
    ffn_tile = lambda i: jnp.minimum(i, n_tiles - 1)
    proj_tile = lambda i: jnp.maximum(i - 1, 0)
    regroup = lambda dil: pl.BlockSpec((1, dil, rows // dil, n_att),
                                       lambda i: (proj_tile(i) // per_seq, 0, proj_tile(i) % per_seq, 0))
    by_rows = lambda width: pl.BlockSpec((rows, width), lambda i: (proj_tile(i), 0))
    position = pl.BlockSpec((rows, RET_HEAD_DIM), lambda i: (proj_tile(i) % per_seq, 0))
    return pl.pallas_call(
        _ffn_in_kernel,
        grid=(n_tiles + 1,),
        in_specs=[
            pl.BlockSpec((rows, d), lambda i: (ffn_tile(i), 0)),
            _resident((1, d)),
            _resident((d, d_ff)),
            _resident((d, d_ff)),
            _resident((d_ff, d)),
            _resident((1, d)),
            _resident((d, n_ret + n_att)),
            position,
            position,
            _resident(zeta_rows.shape),
        ],
        out_specs=([pl.BlockSpec((rows, d), lambda i: (ffn_tile(i), 0))] + [by_rows(RET_WIDTH)] * 5
                   + [by_rows(n_att), regroup(d4), regroup(d16)]),
        out_shape=(
            [jax.ShapeDtypeStruct((t, d), F32)]
            + [jax.ShapeDtypeStruct((t, RET_WIDTH), BF16)] * 4
            + [jax.ShapeDtypeStruct((t, RET_WIDTH), F32),
               jax.ShapeDtypeStruct((t, n_att), BF16),
               jax.ShapeDtypeStruct((batch, d4, seq // d4, n_att), BF16),
               jax.ShapeDtypeStruct((batch, d16, seq // d16, n_att), BF16)]),
        scratch_shapes=[pltpu.VMEM((rows, d), BF16), pltpu.VMEM((rows, d_ff), BF16),
                        pltpu.VMEM((n_att // LANES, rows, LANES), F32),
                        pltpu.VMEM((n_att // LANES, d4, rows // d4, LANES), F32),
                        pltpu.VMEM((2, rows, d), BF16)],
        compiler_params=pltpu.CompilerParams(
            dimension_semantics=("arbitrary",), vmem_limit_bytes=VMEM_LIMIT_BYTES),
        name="ffn_in",
    )(x2, g1, wg, wu, wd, gm, win, cos, sin_signed, zeta_rows)


def _retention_tables(seq, tile_rows):
    d = RET_HEAD_DIM
    pos = jnp.arange(seq, dtype=F32)
    inv_freq = ROPE_BASE ** (-jnp.arange(0, d, 2, dtype=F32) / d)
    ang = jnp.repeat(pos[:, None] * inv_freq[None, :], 2, axis=-1)
    cos = jnp.cos(ang)
    sin = jnp.sin(ang)
    even = (jnp.arange(d) % 2) == 0
    sin_signed = jnp.where(even[None, :], -sin, sin)

    c = RET_CHUNK
    log_g = jnp.log(1.0 - 2.0 ** (-5.0 - jnp.arange(RET_HEADS, dtype=F32)))
    idx = jnp.arange(c, dtype=F32)
    rel = idx[:, None] - idx[None, :]
    decay_in = jnp.where(rel >= 0, jnp.exp(log_g[:, None, None] * jnp.maximum(rel, 0.0)), 0.0)
    zeta = jnp.exp(log_g[:, None] * (c - 1 - idx)[None, :])
    xi = jnp.exp(log_g[:, None] * (idx + 1)[None, :])
    chunk_decay = jnp.exp(log_g * c)
    xi_lanes = jnp.broadcast_to(xi[:, :, None], (RET_HEADS, c, d))
    gamma = jnp.broadcast_to(chunk_decay[:, None, None], (RET_HEADS, 8, d))
    zeta_rows = jnp.tile(jnp.repeat(zeta.T, d, axis=1), (tile_rows // c, 1))
    return cos, sin_signed, zeta_rows, decay_in, xi_lanes, gamma


def _retention_kernel(q_ref, k_ref, kz_ref, v_ref, dec_ref, xi_ref, gamma_ref, o_ref, state_ref):
    c = RET_CHUNK
    d = RET_HEAD_DIM

    @pl.when(pl.program_id(1) == 0)
    def _():
        state_ref[...] = jnp.zeros_like(state_ref)

    def chunk(j, carry):
        rows = pl.ds(pl.multiple_of(j * c, c), c)
        for h in range(RET_HEADS):
            lanes = slice(h * d, (h + 1) * d)
            qb = q_ref[0, rows, lanes]
            vb = v_ref[0, rows, lanes]
            scores = _dot_nt(qb, k_ref[0, rows, lanes]) * dec_ref[h]
            state = state_ref[h]
            o_ref[0, rows, lanes] = _dot(scores.astype(BF16), vb) + _dot(qb, state.astype(BF16)) * xi_ref[h]
            state_ref[h] = state * gamma_ref[h, 0:1, :] + _dot_tn(kz_ref[0, rows, lanes], vb)
        return carry

    lax.fori_loop(0, q_ref.shape[1] // c, chunk, 0, unroll=True)


def _retention(rq, rk, rkz, rv, decay_in, xi_lanes, gamma):
    b, s, w = rq.shape
    rows = RET_ROWS
    d = RET_HEAD_DIM
    tile = pl.BlockSpec((1, rows, w), lambda bi, si: (bi, si, 0))
    return pl.pallas_call(
        _retention_kernel,
        grid=(b, s // rows),
        in_specs=[tile, tile, tile, tile,
                  _resident(decay_in.shape), _resident(xi_lanes.shape), _resident(gamma.shape)],
        out_specs=tile,
        out_shape=jax.ShapeDtypeStruct((b, s, w), F32),
        scratch_shapes=[pltpu.VMEM((RET_HEADS, d, d), F32)],
        compiler_params=pltpu.CompilerParams(
            dimension_semantics=("arbitrary", "arbitrary"), vmem_limit_bytes=VMEM_LIMIT_BYTES),
        name="retention",
    )(rq, rk, rkz, rv, decay_in, xi_lanes, gamma)


ATT_BLOCKS = ATT_ROWS // BAND_BLOCK
KIND_ACC, KIND_M, KIND_L = 0, 1, 2
N_KINDS = 3


def _attention_bias():
    blk = BAND_BLOCK
    qi = np.arange(blk)[:, None]
    kj = np.arange(2 * blk)[None, :]
    dist = qi + blk - kj
    band = (dist >= 0) & (dist <= ATT_SPAN)
    masks = np.stack([band, band & (kj >= blk)])
    return jnp.asarray(np.where(masks, 0.0, -np.inf).astype(np.float32))


def _dilated_kernel(tiles_per_seq, *refs):
    res_ref, y_ref = refs[-2:]
    vbufs = refs[-7:-4]
    step = pl.program_id(0)
    n_tiles = pl.num_programs(0) - 1
    tile = jnp.minimum(step, n_tiles - 1) % tiles_per_seq

    @pl.when(step == 0)
    def _():
        for vbuf in vbufs:
            vbuf[:, LANES:2 * LANES] = jnp.ones((vbuf.shape[0], LANES), BF16)
        res_ref[1] = jnp.ones(res_ref.shape[1:], res_ref.dtype)
        y_ref[1] = jnp.ones(y_ref.shape[1:], y_ref.dtype)

    for slot in range(2):
        pl.when(jnp.bitwise_and(step, 1) == slot)(functools.partial(_dilated_body, slot, tile, *refs))


def _dilated_body(slot, tile, q1, k1c, k1p, v1c, v1p, q4, k4c, k4p, v4c, v4p, q16, k16c, k16p, v16c, v16p,
                  bias_ref, o_ref, kb0, kb1, kb2, vb0, vb1, vb2, s_ref, m_ref, res_ref, y_ref):
    rows = ATT_ROWS
    blk = BAND_BLOCK
    hd = ATT_HEAD_DIM
    n_br = len(DILATIONS)
    kbufs, vbufs = (kb0, kb1, kb2), (vb0, vb1, vb2)
    d4, d16 = DILATIONS[1], DILATIONS[2]
    res_new, y_new = res_ref.at[slot], y_ref.at[slot]
    res_old, y_old = res_ref.at[1 - slot], y_ref.at[1 - slot]

    per_branch = (
        ([k1p.at[0]], [k1c.at[0]], [v1p.at[0]], [v1c.at[0]]),
        ([k4p.at[0, r] for r in range(d4)], [k4c.at[0, r] for r in range(d4)],
         [v4p.at[0, r] for r in range(d4)], [v4c.at[0, r] for r in range(d4)]),
        ([k16p.at[0, r] for r in range(d16)], [k16c.at[0, r] for r in range(d16)],
         [v16p.at[0, r] for r in range(d16)], [v16c.at[0, r] for r in range(d16)]),
    )
    for br, (kps, kcs, vps, vcs) in enumerate(per_branch):
        cur = rows // len(kps)
        for r in range(len(kps)):
            base = r * (blk + cur)
            kbufs[br][base:base + blk, :] = kps[r][...]
            kbufs[br][base + blk:base + blk + cur, :] = kcs[r][...]
            vbufs[br][base:base + blk, 0:LANES] = vps[r][...]
            vbufs[br][base + blk:base + blk + cur, 0:LANES] = vcs[r][...]

    head0 = lax.broadcasted_iota(jnp.int32, (blk, LANES), 1) < hd

    per_res = (ATT_BLOCKS, ATT_BLOCKS // d4, ATT_BLOCKS // d16)
    residue = lambda br, idx: idx // per_res[br]
    block_in_residue = lambda br, idx: idx % per_res[br]
    key_rows = lambda br, idx: slice((idx + residue(br, idx)) * blk, (idx + residue(br, idx) + 2) * blk)

    def load_q(br, idx):
        res, jb = residue(br, idx), block_in_residue(br, idx)
        if br == 0:
            return q1[0, jb * blk:(jb + 1) * blk, :]
        return (q4, q16)[br - 1][0, res, jb * blk:(jb + 1) * blk, :]

    def store_result(br, idx, kind, val):
        res, jb = residue(br, idx), block_in_residue(br, idx)
        if br == 0:
            res_new[kind, idx * blk:(idx + 1) * blk, :] = val
        elif br == 1:
            res_new[N_KINDS + kind, pl.ds(res + d4 * blk * jb, blk, stride=d4), :] = val
        else:
            y_new[kind, pl.ds((res % d4) * (rows // d4) + res // d4, blk, stride=d4), :] = val

    def pair(lo, hi):
        return jnp.where(head0, lo, hi)

    def scores(idx):
        slot = idx % 2
        for br in range(n_br):
            q = load_q(br, idx)
            zero = jnp.zeros_like(q)
            q2 = jnp.concatenate([jnp.where(head0, q, zero), jnp.where(head0, zero, q)], axis=0)
            s = _dot_nt(q2, kbufs[br][key_rows(br, idx), :])
            first = (tile == 0).astype(jnp.int32) if block_in_residue(br, idx) == 0 else 0
            bias = bias_ref[first]
            maxima = []
            for h in range(2):
                r = slice(h * blk, (h + 1) * blk)
                sh = s[r] + bias
                s_ref[slot, br, r, :] = sh
                m = jnp.broadcast_to(jnp.max(sh, axis=-1, keepdims=True), (blk, LANES))
                m_ref[slot, br, h] = m
                maxima.append(m)
            store_result(br, idx, KIND_M, pair(*maxima))

    def values(idx):
        slot = idx % 2
        for br in range(n_br):
            parts = []
            for h in range(2):
                m = m_ref[slot, br, h]
                s = s_ref[slot, br, h * blk:(h + 1) * blk, :]
                parts.append(jnp.exp(s - jnp.concatenate([m, m], axis=1)).astype(BF16))
            pv = _dot(jnp.concatenate(parts, axis=0), vbufs[br][key_rows(br, idx), :])
            store_result(br, idx, KIND_ACC, pair(pv[0:blk, 0:LANES], pv[blk:2 * blk, 0:LANES]))
            store_result(br, idx, KIND_L, pair(pv[0:blk, LANES:2 * LANES], pv[blk:2 * blk, LANES:2 * LANES]))

    sub = rows // d4

    def regroup_old(c):
        for kind in range(N_KINDS):
            for a in range(d4):
                res_old[2 * N_KINDS + kind, pl.ds(a + d4 * blk * c, blk, stride=d4), :] = (
                    y_old[kind, a * sub + c * blk:a * sub + (c + 1) * blk, :])

    def combine_old(i):
        r = slice(i * blk, (i + 1) * blk)
        ms = [res_old[br * N_KINDS + KIND_M, r, :] for br in range(n_br)]
        mx = jnp.maximum(jnp.maximum(ms[0], ms[1]), ms[2])
        ws = [jnp.exp(m - mx) for m in ms]
        num = sum(w * res_old[br * N_KINDS + KIND_ACC, r, :] for br, w in enumerate(ws))
        den = sum(w * res_old[br * N_KINDS + KIND_L, r, :] for br, w in enumerate(ws))
        o_ref[0, r, :] = (num / den).astype(o_ref.dtype)

    scores(0)
    scores(1)
    for i in range(ATT_BLOCKS):
        values(i)
        if i + 2 < ATT_BLOCKS:
            scores(i + 2)
    for i in range(ATT_BLOCKS):
        if i % d4 == 0:
            regroup_old(i // d4)
        combine_old(i)


def _dilated(a1, a4, a16, bias):
    b, s, _ = a1.shape
    rows = ATT_ROWS
    blk = BAND_BLOCK
    pairs = ATT_WIDTH // LANES
    d4, d16 = DILATIONS[1], DILATIONS[2]
    tiles = s // rows
    n_steps = b * pairs * tiles
    decode = lambda i: (i // (pairs * tiles), (i // tiles) % pairs, i % tiles)
    of_compute = lambda f: (lambda i: f(*decode(jnp.minimum(i, n_steps - 1))))
    of_combine = lambda f: (lambda i: f(*decode(jnp.maximum(i - 1, 0))))
    prev_of = lambda n: (lambda i: jnp.maximum(i * n - 1, 0))

    def specs(dil):
        sub = rows // dil
        per = sub // blk
        if dil == 1:
            cur = lambda off: pl.BlockSpec((1, sub, LANES), of_compute(lambda bi, pi, ti: (bi, ti, off + pi)))
            prv = lambda off: pl.BlockSpec(
                (1, blk, LANES), of_compute(lambda bi, pi, ti: (bi, prev_of(per)(ti), off + pi)))
        else:
            cur = lambda off: pl.BlockSpec(
                (1, dil, sub, LANES), of_compute(lambda bi, pi, ti: (bi, 0, ti, off + pi)))
            prv = lambda off: pl.BlockSpec(
                (1, dil, blk, LANES), of_compute(lambda bi, pi, ti: (bi, 0, prev_of(per)(ti), off + pi)))
        return [cur(0), cur(pairs), prv(pairs), cur(2 * pairs), prv(2 * pairs)]

    key_rows = lambda dil: (ATT_BLOCKS + dil) * blk
    return pl.pallas_call(
        functools.partial(_dilated_kernel, tiles),
        grid=(n_steps + 1,),
        in_specs=specs(1) + specs(d4) + specs(d16) + [_resident(bias.shape)],
        out_specs=pl.BlockSpec((1, rows, LANES), of_combine(lambda bi, pi, ti: (bi, ti, pi))),
        out_shape=jax.ShapeDtypeStruct((b, s, ATT_WIDTH), BF16),
        scratch_shapes=(
            [pltpu.VMEM((key_rows(dil), LANES), BF16) for dil in DILATIONS]
            + [pltpu.VMEM((key_rows(dil), 2 * LANES), BF16) for dil in DILATIONS]
            + [pltpu.VMEM((2, len(DILATIONS), 2 * blk, 2 * blk), F32),
               pltpu.VMEM((2, len(DILATIONS), 2, blk, LANES), F32),
               pltpu.VMEM((2, len(DILATIONS) * N_KINDS, rows, LANES), F32),
               pltpu.VMEM((2, N_KINDS, rows, LANES), F32)]),
        compiler_params=pltpu.CompilerParams(
            dimension_semantics=("arbitrary",), vmem_limit_bytes=VMEM_LIMIT_BYTES),
        name="dilated",
    )(a1, a1, a1, a1, a1, a4, a4, a4, a4, a4, a16, a16, a16, a16, a16, bias)


def _out_ffn_kernel(h_ref, ret_ref, gate_ref, att_ref, gain_ref, wo_ref, g2_ref, wg_ref, wu_ref, wd_ref, gf_ref,
                    y_ref, xn_ref, act_ref, mix_ref, h2_ref, *, final_norm):
    d = RET_HEAD_DIM
    step = pl.program_id(0)

    @pl.when(step == 0)
    def _():
        h2_ref[1] = jnp.zeros(h2_ref.shape[1:], h2_ref.dtype)

    def body(slot):
        h2_ref[slot] = h_ref[...] + _dot(att_ref[...], wo_ref[RET_WIDTH:, :])
        h_prev = h2_ref[1 - slot]
        xn_ref[...] = _rms_norm(h_prev, g2_ref[...]).astype(BF16)
        for hd in range(RET_HEADS):
            lanes = slice(hd * d, (hd + 1) * d)
            ret = ret_ref[:, lanes]
            cen = ret - jnp.mean(ret, axis=-1, keepdims=True)
            var = jnp.mean(cen * cen, axis=-1, keepdims=True)
            y = cen * lax.rsqrt(var + GN_EPS) * gain_ref[:, lanes]
            gate = gate_ref[:, lanes]
            mix_ref[:, lanes] = (y * (gate * jax.nn.sigmoid(gate))).astype(BF16)
        out = h2_ref[1 - slot] + 0.5 * _swiglu_half_step(xn_ref, wg_ref, wu_ref, wd_ref, act_ref)
        h2_ref[slot] = h2_ref[slot] + _dot(mix_ref[...], wo_ref[0:RET_WIDTH, :])
        if final_norm:
            out = _rms_norm(out, gf_ref[...])
        y_ref[...] = out

    for slot in range(2):
        pl.when(jnp.bitwise_and(step, 1) == slot)(functools.partial(body, slot))


def _out_ffn(h1, ret, gate, att, gain, wo, g2, wg, wu, wd, gf, final_norm):
    t, d = h1.shape
    d_ff = wg.shape[1]
    rows = FFN_ROWS
    n_tiles = t // rows
    by_rows = lambda a: pl.BlockSpec((rows, a.shape[1]), lambda i: (jnp.minimum(i, n_tiles - 1), 0))
    return pl.pallas_call(
        functools.partial(_out_ffn_kernel, final_norm=final_norm),
        grid=(n_tiles + 1,),
        in_specs=[
            by_rows(h1), by_rows(ret), by_rows(gate), by_rows(att),
            _resident(gain.shape),
            _resident(wo.shape),
            _resident((1, d)),
            _resident((d, d_ff)),
            _resident((d, d_ff)),
            _resident((d_ff, d)),
            _resident((1, d)),
        ],
        out_specs=pl.BlockSpec((rows, d), lambda i: (jnp.maximum(i - 1, 0), 0)),
        out_shape=jax.ShapeDtypeStruct((t, d), F32),
        scratch_shapes=[pltpu.VMEM((rows, d), BF16), pltpu.VMEM((rows, d_ff), BF16),
                        pltpu.VMEM((rows, RET_WIDTH), BF16), pltpu.VMEM((2, rows, d), F32)],
        compiler_params=pltpu.CompilerParams(
            dimension_semantics=("arbitrary",), vmem_limit_bytes=VMEM_LIMIT_BYTES),
        name="out_ffn",
    )(h1, ret, gate, att, gain, wo, g2, wg, wu, wd, gf)


def kernel(x, norm_ffn1, ffn1_w_gate, ffn1_w_up, ffn1_w_down, norm_mix, w_in, ret_norm_gain,
           w_out, norm_ffn2, ffn2_w_gate, ffn2_w_up, ffn2_w_down, norm_final):
    b, s, d = x.shape
    depth = norm_ffn1.shape[0]
    assert s % ATT_ROWS == 0 and s % RET_ROWS == 0 and (b * s) % FFN_ROWS == 0
    assert w_in.shape[2] == 4 * RET_WIDTH + 3 * ATT_WIDTH
    assert FFN_ROWS % RET_CHUNK == 0 and RET_ROWS % RET_CHUNK == 0
    cos, sin_signed, zeta_rows, decay_in, xi_lanes, gamma = _retention_tables(s, FFN_ROWS)
    bias = _attention_bias()
    row = lambda v: v.reshape(1, -1)
    bf = lambda w: w.astype(BF16)
    seq = lambda a: a.reshape(b, s, -1)

    h = x.reshape(b * s, d)
    for l in range(depth):
        h1, rq, rk, rkz, rv, rg, a1, a4, a16 = _ffn_in(
            h, row(norm_ffn1[l]), bf(ffn1_w_gate[l]), bf(ffn1_w_up[l]), bf(ffn1_w_down[l]),
            row(norm_mix[l]), bf(w_in[l]), cos, sin_signed, zeta_rows, b)
        ret = _retention(seq(rq), seq(rk), seq(rkz), seq(rv), decay_in, xi_lanes, gamma)
        att = _dilated(seq(a1), a4, a16, bias)
        h = _out_ffn(h1, ret.reshape(b * s, -1), rg, att.reshape(b * s, -1), row(ret_norm_gain[l]),
                     bf(w_out[l]), row(norm_ffn2[l]), bf(ffn2_w_gate[l]), bf(ffn2_w_up[l]),
                     bf(ffn2_w_down[l]), row(norm_final), final_norm=(l == depth - 1))
    return h.reshape(b, s, d)
```

```python
import functools

import jax
import jax.numpy as jnp
import numpy as np
from jax import lax
from jax.experimental import pallas as pl
from jax.experimental.pallas import tpu as pltpu

RET_HEADS = 4
RET_HEAD_DIM = 128
RET_WIDTH = RET_HEADS * RET_HEAD_DIM
ATT_HEADS = 8
ATT_HEAD_DIM = 64
ATT_WIDTH = ATT_HEADS * ATT_HEAD_DIM
DILATIONS = (1, 4, 16)
ATT_SPAN = 128
BAND_BLOCK = 128
ROPE_BASE = 10000.0
NORM_EPS = 1e-6
GN_EPS = 1e-6

LANES = 128
MXU_DIM = 256
VMEM_LIMIT_BYTES = 56 * 1024 * 1024
WEIGHT_CHUNK_ROWS = 128

FFN_ROWS = 256
FF_CHUNK = MXU_DIM
RET_ROWS = 2048
RET_CHUNK = MXU_DIM
ATT_ROWS = ATT_SPAN * DILATIONS[-1]

F32 = jnp.float32
BF16 = jnp.bfloat16


def _dot(a, b):
    return jnp.dot(a, b, preferred_element_type=F32)


def _dot_nt(a, b):
    return lax.dot_general(a, b, (((1,), (1,)), ((), ())), preferred_element_type=F32)


def _dot_tn(a, b):
    return lax.dot_general(a, b, (((0,), (0,)), ((), ())), preferred_element_type=F32)


def _rms_norm(x, gain):
    return x * lax.rsqrt(jnp.mean(x * x, axis=-1, keepdims=True) + NORM_EPS) * gain


def _swiglu_half_step(xn_ref, wg_ref, wu_ref, wd_ref, act_ref, rows=slice(None)):
    d_ff = wg_ref.shape[1]
    for c in range(d_ff // FF_CHUNK):
        cols = slice(c * FF_CHUNK, (c + 1) * FF_CHUNK)
        g = _dot(xn_ref[rows, :], wg_ref[:, cols])
        u = _dot(xn_ref[rows, :], wu_ref[:, cols])
        act_ref[rows, cols] = (g * jax.nn.sigmoid(g) * u).astype(BF16)
    return _dot(act_ref[rows, :], wd_ref[...])


def _run_lagged(body, n_steps):
    step = pl.program_id(0)
    last = n_steps - 1
    for slot in range(2):
        pl.when((jnp.bitwise_and(step, 1) == slot) & (step < last))(functools.partial(body, slot, True))
    pl.when(step == last)(functools.partial(body, last % 2, False))


def _load_weights_bf16(pairs, stage_ref, sem):
    rows = WEIGHT_CHUNK_ROWS
    chunks = [(src, dst, c) for src, dst in pairs for c in range(src.shape[0] // rows)]

    def copy(k):
        src, _, c = chunks[k]
        return pltpu.make_async_copy(src.at[c * rows:(c + 1) * rows, :],
                                     stage_ref.at[k % 2, :, 0:src.shape[1]], sem.at[k % 2])

    copy(0).start()
    for k, (src, dst, c) in enumerate(chunks):
        if k + 1 < len(chunks):
            copy(k + 1).start()
        copy(k).wait()
        dst[c * rows:(c + 1) * rows, :] = stage_ref[k % 2, :, 0:src.shape[1]].astype(BF16)


def _weight_scratch(*weights):
    widest = max(w.shape[1] for w in weights)
    assert all(w.dtype == F32 and w.shape[0] % WEIGHT_CHUNK_ROWS == 0 for w in weights)
    return ([pltpu.VMEM(w.shape, BF16) for w in weights]
            + [pltpu.VMEM((2, WEIGHT_CHUNK_ROWS, widest), F32), pltpu.SemaphoreType.DMA((2,))])


def _resident(shape):
    return pl.BlockSpec(shape, lambda *_: (0,) * len(shape), pipeline_mode=pl.Buffered(1))


def _rotary(t, cos, sin_signed, even_lane):
    d = t.shape[-1]
    swapped = jnp.where(even_lane, pltpu.roll(t, d - 1, 1), pltpu.roll(t, 1, 1))
    return t * cos + swapped * sin_signed


def _ffn_in_kernel(n_steps, x_ref, g1_ref, wg_hbm, wu_hbm, wd_hbm, gm_ref, win_hbm, cos_ref, sin_ref, zeta_ref,
                   h_ref, rq_ref, rk_ref, rkz_ref, rv_ref, rg_ref, a1_ref, a4_ref, a16_ref,
                   xn_ref, act_ref, slab_ref, by4_ref, hn_ref, wg_ref, wu_ref, wd_ref, win_ref, stage_ref, sem):
    @pl.when(pl.program_id(0) == 0)
    def _():
        _load_weights_bf16([(wg_hbm, wg_ref), (wu_hbm, wu_ref), (wd_hbm, wd_ref), (win_hbm, win_ref)],
                           stage_ref, sem)
        hn_ref[1] = jnp.zeros(hn_ref.shape[1:], hn_ref.dtype)

    def body(slot, lead):
        prev = hn_ref.at[1 - slot]
        _project_attention(prev, win_ref, a1_ref, slab_ref)
        _regroup_attention(slab_ref, by4_ref, a4_ref, a16_ref)
        if lead:
            x = x_ref[...]
            xn_ref[...] = _rms_norm(x, g1_ref[...]).astype(BF16)
            h = x + 0.5 * _swiglu_half_step(xn_ref, wg_ref, wu_ref, wd_ref, act_ref)
            h_ref[...] = h
            hn_ref[slot] = _rms_norm(h, gm_ref[...]).astype(BF16)
        _project_retention_qk(prev, win_ref, cos_ref, sin_ref, zeta_ref, rq_ref, rk_ref, rkz_ref)
        _project_retention_vg(prev, win_ref, rv_ref, rg_ref)

    _run_lagged(body, n_steps)


def _project_retention_qk(xn_ref, win_ref, cos_ref, sin_ref, zeta_ref, rq_ref, rk_ref, rkz_ref):
    rows = xn_ref.shape[0]
    hd = RET_HEAD_DIM
    even_lane = (lax.broadcasted_iota(jnp.int32, (rows, hd), 1) % 2) == 0
    cos = cos_ref[...]
    sin = sin_ref[...]
    for c in range(RET_WIDTH // MXU_DIM):
        cols = slice(c * MXU_DIM, (c + 1) * MXU_DIM)
        uq = _dot(xn_ref[...], win_ref[:, cols])
        uk = _dot(xn_ref[...], win_ref[:, RET_WIDTH + c * MXU_DIM:RET_WIDTH + (c + 1) * MXU_DIM])
        for half in range(MXU_DIM // hd):
            lanes = slice(c * MXU_DIM + half * hd, c * MXU_DIM + (half + 1) * hd)
            part = slice(half * hd, (half + 1) * hd)
            rq_ref[:, lanes] = _rotary(uq[:, part], cos, sin, even_lane).astype(BF16)
            k = _rotary(uk[:, part], cos, sin, even_lane) * (RET_HEAD_DIM ** -0.5)
            rk_ref[:, lanes] = k.astype(BF16)
            rkz_ref[:, lanes] = (k * zeta_ref[:, lanes]).astype(BF16)


def _project_attention(xn_ref, win_ref, a1_ref, slab_ref):
    n_ret = 4 * RET_WIDTH
    for c in range(slab_ref.shape[0] * LANES // MXU_DIM):
        cols = slice(c * MXU_DIM, (c + 1) * MXU_DIM)
        u = _dot(xn_ref[...], win_ref[:, n_ret + c * MXU_DIM:n_ret + (c + 1) * MXU_DIM])
        if (c + 1) * MXU_DIM <= ATT_WIDTH:
            u = u * (ATT_HEAD_DIM ** -0.5)
        a1_ref[:, cols] = u.astype(BF16)
        for half in range(MXU_DIM // LANES):
            slab_ref[c * (MXU_DIM // LANES) + half] = u[:, half * LANES:(half + 1) * LANES]


def _regroup_attention(slab_ref, by4_ref, a4_ref, a16_ref):
    rows = slab_ref.shape[1]
    d4, d16 = DILATIONS[1], DILATIONS[2]
    for s in range(slab_ref.shape[0]):
        lanes = slice(s * LANES, (s + 1) * LANES)
        slab = slab_ref.at[s]
        by4 = by4_ref.at[s]
        for a in range(d4):
            by4[a] = slab[pl.ds(a, rows // d4, stride=d4), :]
            a4_ref[0, a, :, lanes] = by4[a].astype(BF16)
        for res in range(d16):
            a16_ref[0, res, :, lanes] = (
                by4[res % d4, pl.ds(res // d4, rows // d16, stride=d16 // d4), :].astype(BF16))


def _project_retention_vg(xn_ref, win_ref, rv_ref, rg_ref):
    rv_ref[...] = _dot(xn_ref[...], win_ref[:, 2 * RET_WIDTH:3 * RET_WIDTH]).astype(BF16)
    rg_ref[...] = _dot(xn_ref[...], win_ref[:, 3 * RET_WIDTH:4 * RET_WIDTH])


def _ffn_in(x2, g1, wg, wu, wd, gm, win, cos, sin_signed, zeta_rows, batch):
    t, d = x2.shape
    d_ff = wg.shape[1]
    n_ret = 4 * RET_WIDTH
    n_att = 3 * ATT_WIDTH
    rows = FFN_ROWS
    seq = t // batch
    per_seq = seq // rows
    d4, d16 = DILATIONS[1], DILATIONS[2]
    n_tiles = t // rows
    ffn_tile = lambda i: jnp.minimum(i, n_tiles - 1)
    proj_tile = lambda i: jnp.maximum(i - 1, 0)
    regroup = lambda dil: pl.BlockSpec((1, dil, rows // dil, n_att),
                                       lambda i: (proj_tile(i) // per_seq, 0, proj_tile(i) % per_seq, 0))
    by_rows = lambda width: pl.BlockSpec((rows, width), lambda i: (proj_tile(i), 0))
    position = pl.BlockSpec((rows, RET_HEAD_DIM), lambda i: (proj_tile(i) % per_seq, 0))
    in_hbm = pl.BlockSpec(memory_space=pl.ANY)
    return pl.pallas_call(
        functools.partial(_ffn_in_kernel, n_tiles + 1),
        grid=(n_tiles + 1,),
        in_specs=[
            pl.BlockSpec((rows, d), lambda i: (ffn_tile(i), 0)),
            _resident((1, d)),
            in_hbm, in_hbm, in_hbm,
            _resident((1, d)),
            in_hbm,
            position,
            position,
            _resident(zeta_rows.shape),
        ],
        out_specs=([pl.BlockSpec((rows, d), lambda i: (ffn_tile(i), 0))] + [by_rows(RET_WIDTH)] * 5
                   + [by_rows(n_att), regroup(d4), regroup(d16)]),
        out_shape=(
            [jax.ShapeDtypeStruct((t, d), F32)]
            + [jax.ShapeDtypeStruct((t, RET_WIDTH), BF16)] * 4
            + [jax.ShapeDtypeStruct((t, RET_WIDTH), F32),
               jax.ShapeDtypeStruct((t, n_att), BF16),
               jax.ShapeDtypeStruct((batch, d4, seq // d4, n_att), BF16),
               jax.ShapeDtypeStruct((batch, d16, seq // d16, n_att), BF16)]),
        scratch_shapes=[pltpu.VMEM((rows, d), BF16), pltpu.VMEM((rows, d_ff), BF16),
                        pltpu.VMEM((n_att // LANES, rows, LANES), F32),
                        pltpu.VMEM((n_att // LANES, d4, rows // d4, LANES), F32),
                        pltpu.VMEM((2, rows, d), BF16)] + _weight_scratch(wg, wu, wd, win),
        compiler_params=pltpu.CompilerParams(
            dimension_semantics=("arbitrary",), vmem_limit_bytes=VMEM_LIMIT_BYTES),
        name="ffn_in",
    )(x2, g1, wg, wu, wd, gm, win, cos, sin_signed, zeta_rows)


def _retention_tables(seq, tile_rows):
    d = RET_HEAD_DIM
    pos = jnp.arange(seq, dtype=F32)
    inv_freq = ROPE_BASE ** (-jnp.arange(0, d, 2, dtype=F32) / d)
    ang = jnp.repeat(pos[:, None] * inv_freq[None, :], 2, axis=-1)
    cos = jnp.cos(ang)
    sin = jnp.sin(ang)
    even = (jnp.arange(d) % 2) == 0
    sin_signed = jnp.where(even[None, :], -sin, sin)

    c = RET_CHUNK
    log_g = jnp.log(1.0 - 2.0 ** (-5.0 - jnp.arange(RET_HEADS, dtype=F32)))
    idx = jnp.arange(c, dtype=F32)
    rel = idx[:, None] - idx[None, :]
    decay_in = jnp.where(rel >= 0, jnp.exp(log_g[:, None, None] * jnp.maximum(rel, 0.0)), 0.0)
    zeta = jnp.exp(log_g[:, None] * (c - 1 - idx)[None, :])
    xi = jnp.exp(log_g[:, None] * (idx + 1)[None, :])
    chunk_decay = jnp.exp(log_g * c)
    xi_lanes = jnp.broadcast_to(xi[:, :, None], (RET_HEADS, c, d))
    gamma = jnp.broadcast_to(chunk_decay[:, None, None], (RET_HEADS, 8, d))
    zeta_rows = jnp.tile(jnp.repeat(zeta.T, d, axis=1), (tile_rows // c, 1))
    return cos, sin_signed, zeta_rows, decay_in, xi_lanes, gamma


def _retention_kernel(q_ref, k_ref, kz_ref, v_ref, dec_ref, xi_ref, gamma_ref, o_ref, state_ref):
    c = RET_CHUNK
    d = RET_HEAD_DIM

    @pl.when(pl.program_id(1) == 0)
    def _():
        state_ref[...] = jnp.zeros_like(state_ref)

    def chunk(j, carry):
        rows = pl.ds(pl.multiple_of(j * c, c), c)
        for h in range(RET_HEADS):
            lanes = slice(h * d, (h + 1) * d)
            qb = q_ref[0, rows, lanes]
            vb = v_ref[0, rows, lanes]
            scores = _dot_nt(qb, k_ref[0, rows, lanes]) * dec_ref[h]
            state = state_ref[h]
            o_ref[0, rows, lanes] = _dot(scores.astype(BF16), vb) + _dot(qb, state.astype(BF16)) * xi_ref[h]
            state_ref[h] = state * gamma_ref[h, 0:1, :] + _dot_tn(kz_ref[0, rows, lanes], vb)
        return carry

    lax.fori_loop(0, q_ref.shape[1] // c, chunk, 0, unroll=True)


def _retention(rq, rk, rkz, rv, decay_in, xi_lanes, gamma):
    b, s, w = rq.shape
    rows = RET_ROWS
    d = RET_HEAD_DIM
    tile = pl.BlockSpec((1, rows, w), lambda bi, si: (bi, si, 0))
    return pl.pallas_call(
        _retention_kernel,
        grid=(b, s // rows),
        in_specs=[tile, tile, tile, tile,
                  _resident(decay_in.shape), _resident(xi_lanes.shape), _resident(gamma.shape)],
        out_specs=tile,
        out_shape=jax.ShapeDtypeStruct((b, s, w), F32),
        scratch_shapes=[pltpu.VMEM((RET_HEADS, d, d), F32)],
        compiler_params=pltpu.CompilerParams(
            dimension_semantics=("arbitrary", "arbitrary"), vmem_limit_bytes=VMEM_LIMIT_BYTES),
        name="retention",
    )(rq, rk, rkz, rv, decay_in, xi_lanes, gamma)


ATT_BLOCKS = ATT_ROWS // BAND_BLOCK
KIND_ACC, KIND_M, KIND_L = 0, 1, 2
N_KINDS = 3


def _attention_bias():
    blk = BAND_BLOCK
    qi = np.arange(blk)[:, None]
    kj = np.arange(2 * blk)[None, :]
    dist = qi + blk - kj
    band = (dist >= 0) & (dist <= ATT_SPAN)
    masks = np.stack([band, band & (kj >= blk)])
    return jnp.asarray(np.where(masks, 0.0, -np.inf).astype(np.float32))


def _dilated_kernel(tiles_per_seq, n_steps, *refs):
    res_ref, y_ref = refs[-2:]
    vbufs = refs[-7:-4]
    step = pl.program_id(0)
    tile = step % tiles_per_seq

    @pl.when(step == 0)
    def _():
        for vbuf in vbufs:
            vbuf[:, LANES:2 * LANES] = jnp.ones((vbuf.shape[0], LANES), BF16)
        res_ref[1] = jnp.ones(res_ref.shape[1:], res_ref.dtype)
        y_ref[1] = jnp.ones(y_ref.shape[1:], y_ref.dtype)

    _run_lagged(lambda slot, lead: _dilated_body(slot, lead, tile, *refs), n_steps)


def _dilated_body(slot, lead, tile, q1, k1c, k1p, v1c, v1p, q4, k4c, k4p, v4c, v4p, q16, k16c, k16p, v16c, v16p,
                  bias_ref, o_ref, kb0, kb1, kb2, vb0, vb1, vb2, s_ref, m_ref, res_ref, y_ref):
    rows = ATT_ROWS
    blk = BAND_BLOCK
    hd = ATT_HEAD_DIM
    n_br = len(DILATIONS)
    kbufs, vbufs = (kb0, kb1, kb2), (vb0, vb1, vb2)
    d4, d16 = DILATIONS[1], DILATIONS[2]
    res_new, y_new = res_ref.at[slot], y_ref.at[slot]
    res_old, y_old = res_ref.at[1 - slot], y_ref.at[1 - slot]

    per_branch = (
        ([k1p.at[0]], [k1c.at[0]], [v1p.at[0]], [v1c.at[0]]),
        ([k4p.at[0, r] for r in range(d4)], [k4c.at[0, r] for r in range(d4)],
         [v4p.at[0, r] for r in range(d4)], [v4c.at[0, r] for r in range(d4)]),
        ([k16p.at[0, r] for r in range(d16)], [k16c.at[0, r] for r in range(d16)],
         [v16p.at[0, r] for r in range(d16)], [v16c.at[0, r] for r in range(d16)]),
    )
    for br, (kps, kcs, vps, vcs) in enumerate(per_branch if lead else ()):
        cur = rows // len(kps)
        for r in range(len(kps)):
            base = r * (blk + cur)
            kbufs[br][base:base + blk, :] = kps[r][...]
            kbufs[br][base + blk:base + blk + cur, :] = kcs[r][...]
            vbufs[br][base:base + blk, 0:LANES] = vps[r][...]
            vbufs[br][base + blk:base + blk + cur, 0:LANES] = vcs[r][...]

    head0 = lax.broadcasted_iota(jnp.int32, (blk, LANES), 1) < hd

    per_res = (ATT_BLOCKS, ATT_BLOCKS // d4, ATT_BLOCKS // d16)
    residue = lambda br, idx: idx // per_res[br]
    block_in_residue = lambda br, idx: idx % per_res[br]
    key_rows = lambda br, idx: slice((idx + residue(br, idx)) * blk, (idx + residue(br, idx) + 2) * blk)

    def load_q(br, idx):
        res, jb = residue(br, idx), block_in_residue(br, idx)
        if br == 0:
            return q1[0, jb * blk:(jb + 1) * blk, :]
        return (q4, q16)[br - 1][0, res, jb * blk:(jb + 1) * blk, :]

    def store_result(br, idx, kind, val):
        res, jb = residue(br, idx), block_in_residue(br, idx)
        if br == 0:
            res_new[kind, idx * blk:(idx + 1) * blk, :] = val
        elif br == 1:
            res_new[N_KINDS + kind, pl.ds(res + d4 * blk * jb, blk, stride=d4), :] = val
        else:
            y_new[kind, pl.ds((res % d4) * (rows // d4) + res // d4, blk, stride=d4), :] = val

    def pair(lo, hi):
        return jnp.where(head0, lo, hi)

    def scores(idx):
        slot = idx % 2
        for br in range(n_br):
            q = load_q(br, idx)
            zero = jnp.zeros_like(q)
            q2 = jnp.concatenate([jnp.where(head0, q, zero), jnp.where(head0, zero, q)], axis=0)
            s = _dot_nt(q2, kbufs[br][key_rows(br, idx), :])
            first = (tile == 0).astype(jnp.int32) if block_in_residue(br, idx) == 0 else 0
            bias = bias_ref[first]
            maxima = []
            for h in range(2):
                r = slice(h * blk, (h + 1) * blk)
                sh = s[r] + bias
                s_ref[slot, br, r, :] = sh
                m = jnp.broadcast_to(jnp.max(sh, axis=-1, keepdims=True), (blk, LANES))
                m_ref[slot, br, h] = m
                maxima.append(m)
            store_result(br, idx, KIND_M, pair(*maxima))

    def values(idx):
        slot = idx % 2
        for br in range(n_br):
            parts = []
            for h in range(2):
                m = m_ref[slot, br, h]
                s = s_ref[slot, br, h * blk:(h + 1) * blk, :]
                parts.append(jnp.exp(s - jnp.concatenate([m, m], axis=1)).astype(BF16))
            pv = _dot(jnp.concatenate(parts, axis=0), vbufs[br][key_rows(br, idx), :])
            store_result(br, idx, KIND_ACC, pair(pv[0:blk, 0:LANES], pv[blk:2 * blk, 0:LANES]))
            store_result(br, idx, KIND_L, pair(pv[0:blk, LANES:2 * LANES], pv[blk:2 * blk, LANES:2 * LANES]))

    sub = rows // d4

    def regroup_old(c):
        for kind in range(N_KINDS):
            for a in range(d4):
                res_old[2 * N_KINDS + kind, pl.ds(a + d4 * blk * c, blk, stride=d4), :] = (
                    y_old[kind, a * sub + c * blk:a * sub + (c + 1) * blk, :])

    def combine_old(i):
        r = slice(i * blk, (i + 1) * blk)
        ms = [res_old[br * N_KINDS + KIND_M, r, :] for br in range(n_br)]
        mx = jnp.maximum(jnp.maximum(ms[0], ms[1]), ms[2])
        ws = [jnp.exp(m - mx) for m in ms]
        num = sum(w * res_old[br * N_KINDS + KIND_ACC, r, :] for br, w in enumerate(ws))
        den = sum(w * res_old[br * N_KINDS + KIND_L, r, :] for br, w in enumerate(ws))
        o_ref[0, r, :] = (num / den).astype(o_ref.dtype)

    if lead:
        scores(0)
        scores(1)
        for i in range(ATT_BLOCKS):
            values(i)
            if i + 2 < ATT_BLOCKS:
                scores(i + 2)
    for i in range(ATT_BLOCKS):
        if i % d4 == 0:
            regroup_old(i // d4)
        combine_old(i)


def _dilated(a1, a4, a16, bias):
    b, s, _ = a1.shape
    rows = ATT_ROWS
    blk = BAND_BLOCK
    pairs = ATT_WIDTH // LANES
    d4, d16 = DILATIONS[1], DILATIONS[2]
    tiles = s // rows
    n_steps = b * pairs * tiles
    decode = lambda i: (i // (pairs * tiles), (i // tiles) % pairs, i % tiles)
    of_compute = lambda f: (lambda i: f(*decode(jnp.minimum(i, n_steps - 1))))
    of_combine = lambda f: (lambda i: f(*decode(jnp.maximum(i - 1, 0))))
    prev_of = lambda n: (lambda i: jnp.maximum(i * n - 1, 0))

    def specs(dil):
        sub = rows // dil
        per = sub // blk
        if dil == 1:
            cur = lambda off: pl.BlockSpec((1, sub, LANES), of_compute(lambda bi, pi, ti: (bi, ti, off + pi)))
            prv = lambda off: pl.BlockSpec(
                (1, blk, LANES), of_compute(lambda bi, pi, ti: (bi, prev_of(per)(ti), off + pi)))
        else:
            cur = lambda off: pl.BlockSpec(
                (1, dil, sub, LANES), of_compute(lambda bi, pi, ti: (bi, 0, ti, off + pi)))
            prv = lambda off: pl.BlockSpec(
                (1, dil, blk, LANES), of_compute(lambda bi, pi, ti: (bi, 0, prev_of(per)(ti), off + pi)))
        return [cur(0), cur(pairs), prv(pairs), cur(2 * pairs), prv(2 * pairs)]

    key_rows = lambda dil: (ATT_BLOCKS + dil) * blk
    return pl.pallas_call(
        functools.partial(_dilated_kernel, tiles, n_steps + 1),
        grid=(n_steps + 1,),
        in_specs=specs(1) + specs(d4) + specs(d16) + [_resident(bias.shape)],
        out_specs=pl.BlockSpec((1, rows, LANES), of_combine(lambda bi, pi, ti: (bi, ti, pi))),
        out_shape=jax.ShapeDtypeStruct((b, s, ATT_WIDTH), BF16),
        scratch_shapes=(
            [pltpu.VMEM((key_rows(dil), LANES), BF16) for dil in DILATIONS]
            + [pltpu.VMEM((key_rows(dil), 2 * LANES), BF16) for dil in DILATIONS]
            + [pltpu.VMEM((2, len(DILATIONS), 2 * blk, 2 * blk), F32),
               pltpu.VMEM((2, len(DILATIONS), 2, blk, LANES), F32),
               pltpu.VMEM((2, len(DILATIONS) * N_KINDS, rows, LANES), F32),
               pltpu.VMEM((2, N_KINDS, rows, LANES), F32)]),
        compiler_params=pltpu.CompilerParams(
            dimension_semantics=("arbitrary",), vmem_limit_bytes=VMEM_LIMIT_BYTES),
        name="dilated",
    )(a1, a1, a1, a1, a1, a4, a4, a4, a4, a4, a16, a16, a16, a16, a16, bias)


def _out_ffn_kernel(h_ref, ret_ref, gate_ref, att_ref, gain_ref, wo_hbm, g2_ref, wg_hbm, wu_hbm, wd_hbm, gf_ref,
                    y_ref, xn_ref, act_ref, mix_ref, h2_ref, wo_ref, wg_ref, wu_ref, wd_ref, stage_ref, sem,
                    *, final_norm, n_steps):
    d = RET_HEAD_DIM

    @pl.when(pl.program_id(0) == 0)
    def _():
        _load_weights_bf16([(wo_hbm, wo_ref), (wg_hbm, wg_ref), (wu_hbm, wu_ref), (wd_hbm, wd_ref)],
                           stage_ref, sem)
        h2_ref[1] = jnp.zeros(h2_ref.shape[1:], h2_ref.dtype)

    def body(slot, lead):
        if lead:
            h2_ref[slot] = h_ref[...] + _dot(att_ref[...], wo_ref[RET_WIDTH:, :])
        h_prev = h2_ref[1 - slot]
        xn_ref[...] = _rms_norm(h_prev, g2_ref[...]).astype(BF16)
        if lead:
            for hd in range(RET_HEADS):
                lanes = slice(hd * d, (hd + 1) * d)
                ret = ret_ref[:, lanes]
                cen = ret - jnp.mean(ret, axis=-1, keepdims=True)
                var = jnp.mean(cen * cen, axis=-1, keepdims=True)
                y = cen * lax.rsqrt(var + GN_EPS) * gain_ref[:, lanes]
                gate = gate_ref[:, lanes]
                mix_ref[:, lanes] = (y * (gate * jax.nn.sigmoid(gate))).astype(BF16)
        out = h2_ref[1 - slot] + 0.5 * _swiglu_half_step(xn_ref, wg_ref, wu_ref, wd_ref, act_ref)
        if lead:
            h2_ref[slot] = h2_ref[slot] + _dot(mix_ref[...], wo_ref[0:RET_WIDTH, :])
        if final_norm:
            out = _rms_norm(out, gf_ref[...])
        y_ref[...] = out

    _run_lagged(body, n_steps)


def _out_ffn(h1, ret, gate, att, gain, wo, g2, wg, wu, wd, gf, final_norm):
    t, d = h1.shape
    d_ff = wg.shape[1]
    rows = FFN_ROWS
    n_tiles = t // rows
    by_rows = lambda a: pl.BlockSpec((rows, a.shape[1]), lambda i: (jnp.minimum(i, n_tiles - 1), 0))
    in_hbm = pl.BlockSpec(memory_space=pl.ANY)
    return pl.pallas_call(
        functools.partial(_out_ffn_kernel, final_norm=final_norm, n_steps=n_tiles + 1),
        grid=(n_tiles + 1,),
        in_specs=[
            by_rows(h1), by_rows(ret), by_rows(gate), by_rows(att),
            _resident(gain.shape),
            in_hbm,
            _resident((1, d)),
            in_hbm, in_hbm, in_hbm,
            _resident((1, d)),
        ],
        out_specs=pl.BlockSpec((rows, d), lambda i: (jnp.maximum(i - 1, 0), 0)),
        out_shape=jax.ShapeDtypeStruct((t, d), F32),
        scratch_shapes=[pltpu.VMEM((rows, d), BF16), pltpu.VMEM((rows, d_ff), BF16),
                        pltpu.VMEM((rows, RET_WIDTH), BF16), pltpu.VMEM((2, rows, d), F32)]
        + _weight_scratch(wo, wg, wu, wd),
        compiler_params=pltpu.CompilerParams(
            dimension_semantics=("arbitrary",), vmem_limit_bytes=VMEM_LIMIT_BYTES),
        name="out_ffn",
    )(h1, ret, gate, att, gain, wo, g2, wg, wu, wd, gf)


def kernel(x, norm_ffn1, ffn1_w_gate, ffn1_w_up, ffn1_w_down, norm_mix, w_in, ret_norm_gain,
           w_out, norm_ffn2, ffn2_w_gate, ffn2_w_up, ffn2_w_down, norm_final):
    b, s, d = x.shape
    depth = norm_ffn1.shape[0]
    assert s % ATT_ROWS == 0 and s % RET_ROWS == 0 and (b * s) % FFN_ROWS == 0
    assert w_in.shape[2] == 4 * RET_WIDTH + 3 * ATT_WIDTH
    assert FFN_ROWS % RET_CHUNK == 0 and RET_ROWS % RET_CHUNK == 0
    cos, sin_signed, zeta_rows, decay_in, xi_lanes, gamma = _retention_tables(s, FFN_ROWS)
    bias = _attention_bias()
    row = lambda v: v.reshape(1, -1)
    seq = lambda a: a.reshape(b, s, -1)

    h = x.reshape(b * s, d)
    for l in range(depth):
        h1, rq, rk, rkz, rv, rg, a1, a4, a16 = _ffn_in(
            h, row(norm_ffn1[l]), ffn1_w_gate[l], ffn1_w_up[l], ffn1_w_down[l],
            row(norm_mix[l]), w_in[l], cos, sin_signed, zeta_rows, b)
        ret = _retention(seq(rq), seq(rk), seq(rkz), seq(rv), decay_in, xi_lanes, gamma)
        att = _dilated(seq(a1), a4, a16, bias)
        h = _out_ffn(h1, ret.reshape(b * s, -1), rg, att.reshape(b * s, -1), row(ret_norm_gain[l]),
                     w_out[l], row(norm_ffn2[l]), ffn2_w_gate[l], ffn2_w_up[l], ffn2_w_down[l],
                     row(norm_final), final_norm=(l == depth - 1))
    return h.reshape(b, s, d)
```

```python
import functools

import jax
import jax.numpy as jnp
import numpy as np
from jax import lax
from jax.experimental import pallas as pl
from jax.experimental.pallas import tpu as pltpu

RET_HEADS = 4
RET_HEAD_DIM = 128
RET_WIDTH = RET_HEADS * RET_HEAD_DIM
ATT_HEADS = 8
ATT_HEAD_DIM = 64
ATT_WIDTH = ATT_HEADS * ATT_HEAD_DIM
DILATIONS = (1, 4, 16)
ATT_SPAN = 128
BAND_BLOCK = 128
ROPE_BASE = 10000.0
NORM_EPS = 1e-6
GN_EPS = 1e-6

LANES = 128
MXU_DIM = 256
VMEM_LIMIT_BYTES = 56 * 1024 * 1024
WEIGHT_CHUNK_ROWS = 128
WEIGHT_STAGES = 4

FFN_ROWS = 256
FF_CHUNK = MXU_DIM
RET_ROWS = 2048
RET_CHUNK = MXU_DIM
ATT_ROWS = ATT_SPAN * DILATIONS[-1]

F32 = jnp.float32
BF16 = jnp.bfloat16


def _dot(a, b):
    return jnp.dot(a, b, preferred_element_type=F32)


def _dot_nt(a, b):
    return lax.dot_general(a, b, (((1,), (1,)), ((), ())), preferred_element_type=F32)


def _dot_tn(a, b):
    return lax.dot_general(a, b, (((0,), (0,)), ((), ())), preferred_element_type=F32)


def _rms_norm(x, gain):
    return x * lax.rsqrt(jnp.mean(x * x, axis=-1, keepdims=True) + NORM_EPS) * gain


def _swiglu_half_step(xn_ref, wg_ref, wu_ref, wd_ref, act_ref, rows=slice(None)):
    d_ff = wg_ref.shape[1]
    for c in range(d_ff // FF_CHUNK):
        cols = slice(c * FF_CHUNK, (c + 1) * FF_CHUNK)
        g = _dot(xn_ref[rows, :], wg_ref[:, cols])
        u = _dot(xn_ref[rows, :], wu_ref[:, cols])
        act_ref[rows, cols] = (g * jax.nn.sigmoid(g) * u).astype(BF16)
    return _dot(act_ref[rows, :], wd_ref[...])


def _run_lagged(body, n_steps):
    step = pl.program_id(0)
    last = n_steps - 1
    for slot in range(2):
        pl.when((jnp.bitwise_and(step, 1) == slot) & (step < last))(functools.partial(body, slot, True))
    pl.when(step == last)(functools.partial(body, last % 2, False))


def _load_weights_bf16(pairs, stage_ref, sem):
    rows = WEIGHT_CHUNK_ROWS
    depth = stage_ref.shape[0]
    chunks = [(src, dst, c) for src, dst in pairs for c in range(src.shape[0] // rows)]

    def copy(k):
        src, _, c = chunks[k]
        return pltpu.make_async_copy(src.at[c * rows:(c + 1) * rows, :],
                                     stage_ref.at[k % depth, :, 0:src.shape[1]], sem.at[k % depth])

    for k in range(min(depth - 1, len(chunks))):
        copy(k).start()
    for k, (src, dst, c) in enumerate(chunks):
        if k + depth - 1 < len(chunks):
            copy(k + depth - 1).start()
        copy(k).wait()
        dst[c * rows:(c + 1) * rows, :] = stage_ref[k % depth, :, 0:src.shape[1]].astype(BF16)


def _weight_scratch(*weights):
    widest = max(w.shape[1] for w in weights)
    assert all(w.dtype == F32 and w.shape[0] % WEIGHT_CHUNK_ROWS == 0 for w in weights)
    return ([pltpu.VMEM(w.shape, BF16) for w in weights]
            + [pltpu.VMEM((WEIGHT_STAGES, WEIGHT_CHUNK_ROWS, widest), F32),
               pltpu.SemaphoreType.DMA((WEIGHT_STAGES,))])


def _resident(shape):
    return pl.BlockSpec(shape, lambda *_: (0,) * len(shape), pipeline_mode=pl.Buffered(1))


def _rotary(t, cos, sin_signed, even_lane):
    d = t.shape[-1]
    swapped = jnp.where(even_lane, pltpu.roll(t, d - 1, 1), pltpu.roll(t, 1, 1))
    return t * cos + swapped * sin_signed


def _ffn_in_kernel(n_steps, x_ref, g1_ref, wg_hbm, wu_hbm, wd_hbm, gm_ref, win_hbm, cos_ref, sin_ref, zeta_ref,
                   h_ref, rq_ref, rk_ref, rkz_ref, rv_ref, rg_ref, a1_ref, a4_ref, a16_ref,
                   xn_ref, act_ref, slab_ref, by4_ref, hn_ref, wg_ref, wu_ref, wd_ref, win_ref, stage_ref, sem):
    @pl.when(pl.program_id(0) == 0)
    def _():
        _load_weights_bf16([(wg_hbm, wg_ref), (wu_hbm, wu_ref), (wd_hbm, wd_ref), (win_hbm, win_ref)],
                           stage_ref, sem)
        hn_ref[1] = jnp.zeros(hn_ref.shape[1:], hn_ref.dtype)

    def body(slot, lead):
        prev = hn_ref.at[1 - slot]
        _project_attention(prev, win_ref, a1_ref, slab_ref)
        _regroup_attention(slab_ref, by4_ref, a4_ref, a16_ref)
        if lead:
            x = x_ref[...]
            xn_ref[...] = _rms_norm(x, g1_ref[...]).astype(BF16)
            h = x + 0.5 * _swiglu_half_step(xn_ref, wg_ref, wu_ref, wd_ref, act_ref)
            h_ref[...] = h
            hn_ref[slot] = _rms_norm(h, gm_ref[...]).astype(BF16)
        _project_retention_qk(prev, win_ref, cos_ref, sin_ref, zeta_ref, rq_ref, rk_ref, rkz_ref)
        _project_retention_vg(prev, win_ref, rv_ref, rg_ref)

    _run_lagged(body, n_steps)


def _project_retention_qk(xn_ref, win_ref, cos_ref, sin_ref, zeta_ref, rq_ref, rk_ref, rkz_ref):
    rows = xn_ref.shape[0]
    hd = RET_HEAD_DIM
    even_lane = (lax.broadcasted_iota(jnp.int32, (rows, hd), 1) % 2) == 0
    cos = cos_ref[...]
    sin = sin_ref[...]
    for c in range(RET_WIDTH // MXU_DIM):
        cols = slice(c * MXU_DIM, (c + 1) * MXU_DIM)
        uq = _dot(xn_ref[...], win_ref[:, cols])
        uk = _dot(xn_ref[...], win_ref[:, RET_WIDTH + c * MXU_DIM:RET_WIDTH + (c + 1) * MXU_DIM])
        for half in range(MXU_DIM // hd):
            lanes = slice(c * MXU_DIM + half * hd, c * MXU_DIM + (half + 1) * hd)
            part = slice(half * hd, (half + 1) * hd)
            rq_ref[:, lanes] = _rotary(uq[:, part], cos, sin, even_lane).astype(BF16)
            k = _rotary(uk[:, part], cos, sin, even_lane) * (RET_HEAD_DIM ** -0.5)
            rk_ref[:, lanes] = k.astype(BF16)
            rkz_ref[:, lanes] = (k * zeta_ref[:, lanes]).astype(BF16)


def _project_attention(xn_ref, win_ref, a1_ref, slab_ref):
    n_ret = 4 * RET_WIDTH
    for c in range(slab_ref.shape[0] * LANES // MXU_DIM):
        cols = slice(c * MXU_DIM, (c + 1) * MXU_DIM)
        u = _dot(xn_ref[...], win_ref[:, n_ret + c * MXU_DIM:n_ret + (c + 1) * MXU_DIM])
        if (c + 1) * MXU_DIM <= ATT_WIDTH:
            u = u * (ATT_HEAD_DIM ** -0.5)
        a1_ref[:, cols] = u.astype(BF16)
        for half in range(MXU_DIM // LANES):
            slab_ref[c * (MXU_DIM // LANES) + half] = u[:, half * LANES:(half + 1) * LANES]


def _regroup_attention(slab_ref, by4_ref, a4_ref, a16_ref):
    rows = slab_ref.shape[1]
    d4, d16 = DILATIONS[1], DILATIONS[2]
    for s in range(slab_ref.shape[0]):
        lanes = slice(s * LANES, (s + 1) * LANES)
        slab = slab_ref.at[s]
        by4 = by4_ref.at[s]
        for a in range(d4):
            by4[a] = slab[pl.ds(a, rows // d4, stride=d4), :]
            a4_ref[0, a, :, lanes] = by4[a].astype(BF16)
        for res in range(d16):
            a16_ref[0, res, :, lanes] = (
                by4[res % d4, pl.ds(res // d4, rows // d16, stride=d16 // d4), :].astype(BF16))


def _project_retention_vg(xn_ref, win_ref, rv_ref, rg_ref):
    rv_ref[...] = _dot(xn_ref[...], win_ref[:, 2 * RET_WIDTH:3 * RET_WIDTH]).astype(BF16)
    rg_ref[...] = _dot(xn_ref[...], win_ref[:, 3 * RET_WIDTH:4 * RET_WIDTH])


def _ffn_in(x2, g1, wg, wu, wd, gm, win, cos, sin_signed, zeta_rows, batch):
    t, d = x2.shape
    d_ff = wg.shape[1]
    n_ret = 4 * RET_WIDTH
    n_att = 3 * ATT_WIDTH
    rows = FFN_ROWS
    seq = t // batch
    per_seq = seq // rows
    d4, d16 = DILATIONS[1], DILATIONS[2]
    n_tiles = t // rows
    ffn_tile = lambda i: jnp.minimum(i, n_tiles - 1)
    proj_tile = lambda i: jnp.maximum(i - 1, 0)
    regroup = lambda dil: pl.BlockSpec((1, dil, rows // dil, n_att),
                                       lambda i: (proj_tile(i) // per_seq, 0, proj_tile(i) % per_seq, 0))
    by_rows = lambda width: pl.BlockSpec((rows, width), lambda i: (proj_tile(i), 0))
    position = pl.BlockSpec((rows, RET_HEAD_DIM), lambda i: (proj_tile(i) % per_seq, 0))
    in_hbm = pl.BlockSpec(memory_space=pl.ANY)
    return pl.pallas_call(
        functools.partial(_ffn_in_kernel, n_tiles + 1),
        grid=(n_tiles + 1,),
        in_specs=[
            pl.BlockSpec((rows, d), lambda i: (ffn_tile(i), 0)),
            _resident((1, d)),
            in_hbm, in_hbm, in_hbm,
            _resident((1, d)),
            in_hbm,
            position,
            position,
            _resident(zeta_rows.shape),
        ],
        out_specs=([pl.BlockSpec((rows, d), lambda i: (ffn_tile(i), 0))] + [by_rows(RET_WIDTH)] * 5
                   + [by_rows(n_att), regroup(d4), regroup(d16)]),
        out_shape=(
            [jax.ShapeDtypeStruct((t, d), F32)]
            + [jax.ShapeDtypeStruct((t, RET_WIDTH), BF16)] * 4
            + [jax.ShapeDtypeStruct((t, RET_WIDTH), F32),
               jax.ShapeDtypeStruct((t, n_att), BF16),
               jax.ShapeDtypeStruct((batch, d4, seq // d4, n_att), BF16),
               jax.ShapeDtypeStruct((batch, d16, seq // d16, n_att), BF16)]),
        scratch_shapes=[pltpu.VMEM((rows, d), BF16), pltpu.VMEM((rows, d_ff), BF16),
                        pltpu.VMEM((n_att // LANES, rows, LANES), F32),
                        pltpu.VMEM((n_att // LANES, d4, rows // d4, LANES), F32),
                        pltpu.VMEM((2, rows, d), BF16)] + _weight_scratch(wg, wu, wd, win),
        compiler_params=pltpu.CompilerParams(
            dimension_semantics=("arbitrary",), vmem_limit_bytes=VMEM_LIMIT_BYTES),
        name="ffn_in",
    )(x2, g1, wg, wu, wd, gm, win, cos, sin_signed, zeta_rows)


def _retention_tables(seq, tile_rows):
    d = RET_HEAD_DIM
    pos = jnp.arange(seq, dtype=F32)
    inv_freq = ROPE_BASE ** (-jnp.arange(0, d, 2, dtype=F32) / d)
    ang = jnp.repeat(pos[:, None] * inv_freq[None, :], 2, axis=-1)
    cos = jnp.cos(ang)
    sin = jnp.sin(ang)
    even = (jnp.arange(d) % 2) == 0
    sin_signed = jnp.where(even[None, :], -sin, sin)

    c = RET_CHUNK
    log_g = jnp.log(1.0 - 2.0 ** (-5.0 - jnp.arange(RET_HEADS, dtype=F32)))
    idx = jnp.arange(c, dtype=F32)
    rel = idx[:, None] - idx[None, :]
    decay_in = jnp.where(rel >= 0, jnp.exp(log_g[:, None, None] * jnp.maximum(rel, 0.0)), 0.0)
    zeta = jnp.exp(log_g[:, None] * (c - 1 - idx)[None, :])
    xi = jnp.exp(log_g[:, None] * (idx + 1)[None, :])
    chunk_decay = jnp.exp(log_g * c)
    xi_lanes = jnp.broadcast_to(xi[:, :, None], (RET_HEADS, c, d))
    gamma = jnp.broadcast_to(chunk_decay[:, None, None], (RET_HEADS, 8, d))
    zeta_rows = jnp.tile(jnp.repeat(zeta.T, d, axis=1), (tile_rows // c, 1))
    return cos, sin_signed, zeta_rows, decay_in, xi_lanes, gamma


def _retention_kernel(q_ref, k_ref, kz_ref, v_ref, dec_ref, xi_ref, gamma_ref, o_ref, state_ref):
    c = RET_CHUNK
    d = RET_HEAD_DIM

    @pl.when(pl.program_id(1) == 0)
    def _():
        state_ref[...] = jnp.zeros_like(state_ref)

    def chunk(j, carry):
        rows = pl.ds(pl.multiple_of(j * c, c), c)
        for h in range(RET_HEADS):
            lanes = slice(h * d, (h + 1) * d)
            qb = q_ref[0, rows, lanes]
            vb = v_ref[0, rows, lanes]
            scores = _dot_nt(qb, k_ref[0, rows, lanes]) * dec_ref[h]
            state = state_ref[h]
            o_ref[0, rows, lanes] = _dot(scores.astype(BF16), vb) + _dot(qb, state.astype(BF16)) * xi_ref[h]
            state_ref[h] = state * gamma_ref[h, 0:1, :] + _dot_tn(kz_ref[0, rows, lanes], vb)
        return carry

    lax.fori_loop(0, q_ref.shape[1] // c, chunk, 0, unroll=True)


def _retention(rq, rk, rkz, rv, decay_in, xi_lanes, gamma):
    b, s, w = rq.shape
    rows = RET_ROWS
    d = RET_HEAD_DIM
    tile = pl.BlockSpec((1, rows, w), lambda bi, si: (bi, si, 0))
    return pl.pallas_call(
        _retention_kernel,
        grid=(b, s // rows),
        in_specs=[tile, tile, tile, tile,
                  _resident(decay_in.shape), _resident(xi_lanes.shape), _resident(gamma.shape)],
        out_specs=tile,
        out_shape=jax.ShapeDtypeStruct((b, s, w), F32),
        scratch_shapes=[pltpu.VMEM((RET_HEADS, d, d), F32)],
        compiler_params=pltpu.CompilerParams(
            dimension_semantics=("arbitrary", "arbitrary"), vmem_limit_bytes=VMEM_LIMIT_BYTES),
        name="retention",
    )(rq, rk, rkz, rv, decay_in, xi_lanes, gamma)


ATT_BLOCKS = ATT_ROWS // BAND_BLOCK
KIND_ACC, KIND_M, KIND_L = 0, 1, 2
N_KINDS = 3


def _attention_bias():
    blk = BAND_BLOCK
    qi = np.arange(blk)[:, None]
    kj = np.arange(2 * blk)[None, :]
    dist = qi + blk - kj
    band = (dist >= 0) & (dist <= ATT_SPAN)
    masks = np.stack([band, band & (kj >= blk)])
    return jnp.asarray(np.where(masks, 0.0, -np.inf).astype(np.float32))


def _dilated_kernel(tiles_per_seq, n_steps, *refs):
    res_ref, y_ref = refs[-2:]
    vbufs = refs[-7:-4]
    step = pl.program_id(0)
    tile = step % tiles_per_seq

    @pl.when(step == 0)
    def _():
        for vbuf in vbufs:
            vbuf[:, LANES:2 * LANES] = jnp.ones((vbuf.shape[0], LANES), BF16)
        res_ref[1] = jnp.ones(res_ref.shape[1:], res_ref.dtype)
        y_ref[1] = jnp.ones(y_ref.shape[1:], y_ref.dtype)

    _run_lagged(lambda slot, lead: _dilated_body(slot, lead, tile, *refs), n_steps)


def _dilated_body(slot, lead, tile, q1, k1c, k1p, v1c, v1p, q4, k4c, k4p, v4c, v4p, q16, k16c, k16p, v16c, v16p,
                  bias_ref, o_ref, kb0, kb1, kb2, vb0, vb1, vb2, s_ref, m_ref, res_ref, y_ref):
    rows = ATT_ROWS
    blk = BAND_BLOCK
    hd = ATT_HEAD_DIM
    n_br = len(DILATIONS)
    kbufs, vbufs = (kb0, kb1, kb2), (vb0, vb1, vb2)
    d4, d16 = DILATIONS[1], DILATIONS[2]
    res_new, y_new = res_ref.at[slot], y_ref.at[slot]
    res_old, y_old = res_ref.at[1 - slot], y_ref.at[1 - slot]

    per_branch = (
        ([k1p.at[0]], [k1c.at[0]], [v1p.at[0]], [v1c.at[0]]),
        ([k4p.at[0, r] for r in range(d4)], [k4c.at[0, r] for r in range(d4)],
         [v4p.at[0, r] for r in range(d4)], [v4c.at[0, r] for r in range(d4)]),
        ([k16p.at[0, r] for r in range(d16)], [k16c.at[0, r] for r in range(d16)],
         [v16p.at[0, r] for r in range(d16)], [v16c.at[0, r] for r in range(d16)]),
    )
    for br, (kps, kcs, vps, vcs) in enumerate(per_branch if lead else ()):
        cur = rows // len(kps)
        for r in range(len(kps)):
            base = r * (blk + cur)
            kbufs[br][base:base + blk, :] = kps[r][...]
            kbufs[br][base + blk:base + blk + cur, :] = kcs[r][...]
            vbufs[br][base:base + blk, 0:LANES] = vps[r][...]
            vbufs[br][base + blk:base + blk + cur, 0:LANES] = vcs[r][...]

    head0 = lax.broadcasted_iota(jnp.int32, (blk, LANES), 1) < hd

    per_res = (ATT_BLOCKS, ATT_BLOCKS // d4, ATT_BLOCKS // d16)
    residue = lambda br, idx: idx // per_res[br]
    block_in_residue = lambda br, idx: idx % per_res[br]
    key_rows = lambda br, idx: slice((idx + residue(br, idx)) * blk, (idx + residue(br, idx) + 2) * blk)

    def load_q(br, idx):
        res, jb = residue(br, idx), block_in_residue(br, idx)
        if br == 0:
            return q1[0, jb * blk:(jb + 1) * blk, :]
        return (q4, q16)[br - 1][0, res, jb * blk:(jb + 1) * blk, :]

    def store_result(br, idx, kind, val):
        res, jb = residue(br, idx), block_in_residue(br, idx)
        if br == 0:
            res_new[kind, idx * blk:(idx + 1) * blk, :] = val
        elif br == 1:
            res_new[N_KINDS + kind, pl.ds(res + d4 * blk * jb, blk, stride=d4), :] = val
        else:
            y_new[kind, pl.ds((res % d4) * (rows // d4) + res // d4, blk, stride=d4), :] = val

    def pair(lo, hi):
        return jnp.where(head0, lo, hi)

    def scores(idx):
        slot = idx % 2
        for br in range(n_br):
            q = load_q(br, idx)
            zero = jnp.zeros_like(q)
            q2 = jnp.concatenate([jnp.where(head0, q, zero), jnp.where(head0, zero, q)], axis=0)
            s = _dot_nt(q2, kbufs[br][key_rows(br, idx), :])
            first = (tile == 0).astype(jnp.int32) if block_in_residue(br, idx) == 0 else 0
            bias = bias_ref[first]
            maxima = []
            for h in range(2):
                r = slice(h * blk, (h + 1) * blk)
                sh = s[r] + bias
                s_ref[slot, br, r, :] = sh
                m = jnp.broadcast_to(jnp.max(sh, axis=-1, keepdims=True), (blk, LANES))
                m_ref[slot, br, h] = m
                maxima.append(m)
            store_result(br, idx, KIND_M, pair(*maxima))

    def values(idx):
        slot = idx % 2
        for br in range(n_br):
            parts = []
            for h in range(2):
                m = m_ref[slot, br, h]
                s = s_ref[slot, br, h * blk:(h + 1) * blk, :]
                parts.append(jnp.exp(s - jnp.concatenate([m, m], axis=1)).astype(BF16))
            pv = _dot(jnp.concatenate(parts, axis=0), vbufs[br][key_rows(br, idx), :])
            store_result(br, idx, KIND_ACC, pair(pv[0:blk, 0:LANES], pv[blk:2 * blk, 0:LANES]))
            store_result(br, idx, KIND_L, pair(pv[0:blk, LANES:2 * LANES], pv[blk:2 * blk, LANES:2 * LANES]))

    sub = rows // d4

    def regroup_old(c):
        for kind in range(N_KINDS):
            for a in range(d4):
                res_old[2 * N_KINDS + kind, pl.ds(a + d4 * blk * c, blk, stride=d4), :] = (
                    y_old[kind, a * sub + c * blk:a * sub + (c + 1) * blk, :])

    def combine_old(i):
        r = slice(i * blk, (i + 1) * blk)
        ms = [res_old[br * N_KINDS + KIND_M, r, :] for br in range(n_br)]
        mx = jnp.maximum(jnp.maximum(ms[0], ms[1]), ms[2])
        ws = [jnp.exp(m - mx) for m in ms]
        num = sum(w * res_old[br * N_KINDS + KIND_ACC, r, :] for br, w in enumerate(ws))
        den = sum(w * res_old[br * N_KINDS + KIND_L, r, :] for br, w in enumerate(ws))
        o_ref[0, r, :] = (num / den).astype(o_ref.dtype)

    if lead:
        scores(0)
        scores(1)
        for i in range(ATT_BLOCKS):
            values(i)
            if i + 2 < ATT_BLOCKS:
                scores(i + 2)
    for i in range(ATT_BLOCKS):
        if i % d4 == 0:
            regroup_old(i // d4)
        combine_old(i)


def _dilated(a1, a4, a16, bias):
    b, s, _ = a1.shape
    rows = ATT_ROWS
    blk = BAND_BLOCK
    pairs = ATT_WIDTH // LANES
    d4, d16 = DILATIONS[1], DILATIONS[2]
    tiles = s // rows
    n_steps = b * pairs * tiles
    decode = lambda i: (i // (pairs * tiles), (i // tiles) % pairs, i % tiles)
    of_compute = lambda f: (lambda i: f(*decode(jnp.minimum(i, n_steps - 1))))
    of_combine = lambda f: (lambda i: f(*decode(jnp.maximum(i - 1, 0))))
    prev_of = lambda n: (lambda i: jnp.maximum(i * n - 1, 0))

    def specs(dil):
        sub = rows // dil
        per = sub // blk
        if dil == 1:
            cur = lambda off: pl.BlockSpec((1, sub, LANES), of_compute(lambda bi, pi, ti: (bi, ti, off + pi)))
            prv = lambda off: pl.BlockSpec(
                (1, blk, LANES), of_compute(lambda bi, pi, ti: (bi, prev_of(per)(ti), off + pi)))
        else:
            cur = lambda off: pl.BlockSpec(
                (1, dil, sub, LANES), of_compute(lambda bi, pi, ti: (bi, 0, ti, off + pi)))
            prv = lambda off: pl.BlockSpec(
                (1, dil, blk, LANES), of_compute(lambda bi, pi, ti: (bi, 0, prev_of(per)(ti), off + pi)))
        return [cur(0), cur(pairs), prv(pairs), cur(2 * pairs), prv(2 * pairs)]

    key_rows = lambda dil: (ATT_BLOCKS + dil) * blk
    return pl.pallas_call(
        functools.partial(_dilated_kernel, tiles, n_steps + 1),
        grid=(n_steps + 1,),
        in_specs=specs(1) + specs(d4) + specs(d16) + [_resident(bias.shape)],
        out_specs=pl.BlockSpec((1, rows, LANES), of_combine(lambda bi, pi, ti: (bi, ti, pi))),
        out_shape=jax.ShapeDtypeStruct((b, s, ATT_WIDTH), BF16),
        scratch_shapes=(
            [pltpu.VMEM((key_rows(dil), LANES), BF16) for dil in DILATIONS]
            + [pltpu.VMEM((key_rows(dil), 2 * LANES), BF16) for dil in DILATIONS]
            + [pltpu.VMEM((2, len(DILATIONS), 2 * blk, 2 * blk), F32),
               pltpu.VMEM((2, len(DILATIONS), 2, blk, LANES), F32),
               pltpu.VMEM((2, len(DILATIONS) * N_KINDS, rows, LANES), F32),
               pltpu.VMEM((2, N_KINDS, rows, LANES), F32)]),
        compiler_params=pltpu.CompilerParams(
            dimension_semantics=("arbitrary",), vmem_limit_bytes=VMEM_LIMIT_BYTES),
        name="dilated",
    )(a1, a1, a1, a1, a1, a4, a4, a4, a4, a4, a16, a16, a16, a16, a16, bias)


def _out_ffn_kernel(h_ref, ret_ref, gate_ref, att_ref, gain_ref, wo_hbm, g2_ref, wg_hbm, wu_hbm, wd_hbm, gf_ref,
                    y_ref, xn_ref, act_ref, mix_ref, h2_ref, wo_ref, wg_ref, wu_ref, wd_ref, stage_ref, sem,
                    *, final_norm, n_steps):
    d = RET_HEAD_DIM

    @pl.when(pl.program_id(0) == 0)
    def _():
        _load_weights_bf16([(wo_hbm, wo_ref), (wg_hbm, wg_ref), (wu_hbm, wu_ref), (wd_hbm, wd_ref)],
                           stage_ref, sem)
        h2_ref[1] = jnp.zeros(h2_ref.shape[1:], h2_ref.dtype)

    def body(slot, lead):
        if lead:
            h2_ref[slot] = h_ref[...] + _dot(att_ref[...], wo_ref[RET_WIDTH:, :])
        h_prev = h2_ref[1 - slot]
        xn_ref[...] = _rms_norm(h_prev, g2_ref[...]).astype(BF16)
        if lead:
            for hd in range(RET_HEADS):
                lanes = slice(hd * d, (hd + 1) * d)
                ret = ret_ref[:, lanes]
                cen = ret - jnp.mean(ret, axis=-1, keepdims=True)
                var = jnp.mean(cen * cen, axis=-1, keepdims=True)
                y = cen * lax.rsqrt(var + GN_EPS) * gain_ref[:, lanes]
                gate = gate_ref[:, lanes]
                mix_ref[:, lanes] = (y * (gate * jax.nn.sigmoid(gate))).astype(BF16)
        out = h2_ref[1 - slot] + 0.5 * _swiglu_half_step(xn_ref, wg_ref, wu_ref, wd_ref, act_ref)
        if lead:
            h2_ref[slot] = h2_ref[slot] + _dot(mix_ref[...], wo_ref[0:RET_WIDTH, :])
        if final_norm:
            out = _rms_norm(out, gf_ref[...])
        y_ref[...] = out

    _run_lagged(body, n_steps)


def _out_ffn(h1, ret, gate, att, gain, wo, g2, wg, wu, wd, gf, final_norm):
    t, d = h1.shape
    d_ff = wg.shape[1]
    rows = FFN_ROWS
    n_tiles = t // rows
    by_rows = lambda a: pl.BlockSpec((rows, a.shape[1]), lambda i: (jnp.minimum(i, n_tiles - 1), 0))
    in_hbm = pl.BlockSpec(memory_space=pl.ANY)
    return pl.pallas_call(
        functools.partial(_out_ffn_kernel, final_norm=final_norm, n_steps=n_tiles + 1),
        grid=(n_tiles + 1,),
        in_specs=[
            by_rows(h1), by_rows(ret), by_rows(gate), by_rows(att),
            _resident(gain.shape),
            in_hbm,
            _resident((1, d)),
            in_hbm, in_hbm, in_hbm,
            _resident((1, d)),
        ],
        out_specs=pl.BlockSpec((rows, d), lambda i: (jnp.maximum(i - 1, 0), 0)),
        out_shape=jax.ShapeDtypeStruct((t, d), F32),
        scratch_shapes=[pltpu.VMEM((rows, d), BF16), pltpu.VMEM((rows, d_ff), BF16),
                        pltpu.VMEM((rows, RET_WIDTH), BF16), pltpu.VMEM((2, rows, d), F32)]
        + _weight_scratch(wo, wg, wu, wd),
        compiler_params=pltpu.CompilerParams(
            dimension_semantics=("arbitrary",), vmem_limit_bytes=VMEM_LIMIT_BYTES),
        name="out_ffn",
    )(h1, ret, gate, att, gain, wo, g2, wg, wu, wd, gf)


def kernel(x, norm_ffn1, ffn1_w_gate, ffn1_w_up, ffn1_w_down, norm_mix, w_in, ret_norm_gain,
           w_out, norm_ffn2, ffn2_w_gate, ffn2_w_up, ffn2_w_down, norm_final):
    b, s, d = x.shape
    depth = norm_ffn1.shape[0]
    assert s % ATT_ROWS == 0 and s % RET_ROWS == 0 and (b * s) % FFN_ROWS == 0
    assert w_in.shape[2] == 4 * RET_WIDTH + 3 * ATT_WIDTH
    assert FFN_ROWS % RET_CHUNK == 0 and RET_ROWS % RET_CHUNK == 0
    cos, sin_signed, zeta_rows, decay_in, xi_lanes, gamma = _retention_tables(s, FFN_ROWS)
    bias = _attention_bias()
    row = lambda v: v.reshape(1, -1)
    seq = lambda a: a.reshape(b, s, -1)

    h = x.reshape(b * s, d)
    for l in range(depth):
        h1, rq, rk, rkz, rv, rg, a1, a4, a16 = _ffn_in(
            h, row(norm_ffn1[l]), ffn1_w_gate[l], ffn1_w_up[l], ffn1_w_down[l],
            row(norm_mix[l]), w_in[l], cos, sin_signed, zeta_rows, b)
        ret = _retention(seq(rq), seq(rk), seq(rkz), seq(rv), decay_in, xi_lanes, gamma)
        att = _dilated(seq(a1), a4, a16, bias)
        h = _out_ffn(h1, ret.reshape(b * s, -1), rg, att.reshape(b * s, -1), row(ret_norm_gain[l]),
                     w_out[l], row(norm_ffn2[l]), ffn2_w_gate[l], ffn2_w_up[l], ffn2_w_down[l],
                     row(norm_final), final_norm=(l == depth - 1))
    return h.reshape(b, s, d)
```

```python
import functools

import jax
import jax.numpy as jnp
import numpy as np
from jax import lax
from jax.experimental import pallas as pl
from jax.experimental.pallas import tpu as pltpu

RET_HEADS = 4
RET_HEAD_DIM = 128
RET_WIDTH = RET_HEADS * RET_HEAD_DIM
ATT_HEADS = 8
ATT_HEAD_DIM = 64
ATT_WIDTH = ATT_HEADS * ATT_HEAD_DIM
DILATIONS = (1, 4, 16)
ATT_SPAN = 128
BAND_BLOCK = 128
ROPE_BASE = 10000.0
NORM_EPS = 1e-6
GN_EPS = 1e-6

LANES = 128
MXU_DIM = 256
VMEM_LIMIT_BYTES = 56 * 1024 * 1024
WEIGHT_CHUNK_ROWS = 128
WEIGHT_STAGES = 4

FFN_ROWS = 256
FF_CHUNK = MXU_DIM
RET_ROWS = 2048
RET_CHUNK = MXU_DIM
ATT_ROWS = ATT_SPAN * DILATIONS[-1]

F32 = jnp.float32
BF16 = jnp.bfloat16


def _dot(a, b):
    return jnp.dot(a, b, preferred_element_type=F32)


def _dot_nt(a, b):
    return lax.dot_general(a, b, (((1,), (1,)), ((), ())), preferred_element_type=F32)


def _dot_tn(a, b):
    return lax.dot_general(a, b, (((0,), (0,)), ((), ())), preferred_element_type=F32)


def _rms_norm(x, gain):
    return x * lax.rsqrt(jnp.mean(x * x, axis=-1, keepdims=True) + NORM_EPS) * gain


def _swiglu_half_step(xn_ref, wg_ref, wu_ref, wd_ref, act_ref, rows=slice(None)):
    d_ff = wg_ref.shape[1]
    for c in range(d_ff // FF_CHUNK):
        cols = slice(c * FF_CHUNK, (c + 1) * FF_CHUNK)
        g = _dot(xn_ref[rows, :], wg_ref[:, cols])
        u = _dot(xn_ref[rows, :], wu_ref[:, cols])
        act_ref[rows, cols] = (g * jax.nn.sigmoid(g) * u).astype(BF16)
    return _dot(act_ref[rows, :], wd_ref[...])


def _run_lagged(body, n_steps):
    step = pl.program_id(0)
    last = n_steps - 1
    for slot in range(2):
        pl.when((jnp.bitwise_and(step, 1) == slot) & (step < last))(functools.partial(body, slot, True))
    pl.when(step == last)(functools.partial(body, last % 2, False))


def _load_weights_bf16(pairs, stage_ref, sem):
    rows = WEIGHT_CHUNK_ROWS
    depth = stage_ref.shape[0]
    chunks = [(src, dst, c) for src, dst in pairs for c in range(src.shape[0] // rows)]

    def copy(k):
        src, _, c = chunks[k]
        return pltpu.make_async_copy(src.at[c * rows:(c + 1) * rows, :],
                                     stage_ref.at[k % depth, :, 0:src.shape[1]], sem.at[k % depth])

    for k in range(min(depth - 1, len(chunks))):
        copy(k).start()
    for k, (src, dst, c) in enumerate(chunks):
        if k + depth - 1 < len(chunks):
            copy(k + depth - 1).start()
        copy(k).wait()
        dst[c * rows:(c + 1) * rows, :] = stage_ref[k % depth, :, 0:src.shape[1]].astype(BF16)


def _weight_scratch(*weights):
    widest = max(w.shape[1] for w in weights)
    assert all(w.dtype == F32 and w.shape[0] % WEIGHT_CHUNK_ROWS == 0 for w in weights)
    return ([pltpu.VMEM(w.shape, BF16) for w in weights]
            + [pltpu.VMEM((WEIGHT_STAGES, WEIGHT_CHUNK_ROWS, widest), F32),
               pltpu.SemaphoreType.DMA((WEIGHT_STAGES,))])


def _resident(shape):
    return pl.BlockSpec(shape, lambda *_: (0,) * len(shape), pipeline_mode=pl.Buffered(1))


def _rotary(t, cos, sin_signed, even_lane):
    d = t.shape[-1]
    swapped = jnp.where(even_lane, pltpu.roll(t, d - 1, 1), pltpu.roll(t, 1, 1))
    return t * cos + swapped * sin_signed


def _ffn_in_kernel(n_steps, x_ref, g1_ref, wg_hbm, wu_hbm, wd_hbm, gm_ref, win_hbm, rot_ref, zeta_ref,
                   h_ref, r_ref, rg_ref, a1_ref, a4_ref, a16_ref,
                   xn_ref, act_ref, slab_ref, by4_ref, hn_ref, wg_ref, wu_ref, wd_ref, win_ref, stage_ref, sem):
    @pl.when(pl.program_id(0) == 0)
    def _():
        _load_weights_bf16([(wg_hbm, wg_ref), (wu_hbm, wu_ref), (wd_hbm, wd_ref), (win_hbm, win_ref)],
                           stage_ref, sem)
        hn_ref[1] = jnp.zeros(hn_ref.shape[1:], hn_ref.dtype)

    rq_ref, rk_ref, rkz_ref, rv_ref = (r_ref.at[:, j * RET_WIDTH:(j + 1) * RET_WIDTH] for j in range(4))

    def body(slot, lead):
        prev = hn_ref.at[1 - slot]
        _project_attention(prev, win_ref, a1_ref, slab_ref)
        _regroup_attention(slab_ref, by4_ref, a4_ref, a16_ref)
        if lead:
            x = x_ref[...]
            xn_ref[...] = _rms_norm(x, g1_ref[...]).astype(BF16)
            h = x + 0.5 * _swiglu_half_step(xn_ref, wg_ref, wu_ref, wd_ref, act_ref)
            h_ref[...] = h
            hn_ref[slot] = _rms_norm(h, gm_ref[...]).astype(BF16)
        _project_retention_qk(prev, win_ref, rot_ref, zeta_ref, rq_ref, rk_ref, rkz_ref)
        _project_retention_vg(prev, win_ref, rv_ref, rg_ref)

    _run_lagged(body, n_steps)


def _project_retention_qk(xn_ref, win_ref, rot_ref, zeta_ref, rq_ref, rk_ref, rkz_ref):
    rows = xn_ref.shape[0]
    hd = RET_HEAD_DIM
    even_lane = (lax.broadcasted_iota(jnp.int32, (rows, hd), 1) % 2) == 0
    cos = rot_ref[:, 0:hd]
    sin = rot_ref[:, hd:2 * hd]
    for c in range(RET_WIDTH // MXU_DIM):
        cols = slice(c * MXU_DIM, (c + 1) * MXU_DIM)
        uq = _dot(xn_ref[...], win_ref[:, cols])
        uk = _dot(xn_ref[...], win_ref[:, RET_WIDTH + c * MXU_DIM:RET_WIDTH + (c + 1) * MXU_DIM])
        for half in range(MXU_DIM // hd):
            lanes = slice(c * MXU_DIM + half * hd, c * MXU_DIM + (half + 1) * hd)
            part = slice(half * hd, (half + 1) * hd)
            rq_ref[:, lanes] = _rotary(uq[:, part], cos, sin, even_lane).astype(BF16)
            k = _rotary(uk[:, part], cos, sin, even_lane) * (RET_HEAD_DIM ** -0.5)
            rk_ref[:, lanes] = k.astype(BF16)
            rkz_ref[:, lanes] = (k * zeta_ref[:, lanes]).astype(BF16)


def _project_attention(xn_ref, win_ref, a1_ref, slab_ref):
    n_ret = 4 * RET_WIDTH
    for c in range(slab_ref.shape[0] * LANES // MXU_DIM):
        cols = slice(c * MXU_DIM, (c + 1) * MXU_DIM)
        u = _dot(xn_ref[...], win_ref[:, n_ret + c * MXU_DIM:n_ret + (c + 1) * MXU_DIM])
        if (c + 1) * MXU_DIM <= ATT_WIDTH:
            u = u * (ATT_HEAD_DIM ** -0.5)
        a1_ref[:, cols] = u.astype(BF16)
        for half in range(MXU_DIM // LANES):
            slab_ref[c * (MXU_DIM // LANES) + half] = u[:, half * LANES:(half + 1) * LANES]


def _regroup_attention(slab_ref, by4_ref, a4_ref, a16_ref):
    rows = slab_ref.shape[1]
    d4, d16 = DILATIONS[1], DILATIONS[2]
    for s in range(slab_ref.shape[0]):
        lanes = slice(s * LANES, (s + 1) * LANES)
        slab = slab_ref.at[s]
        by4 = by4_ref.at[s]
        for a in range(d4):
            by4[a] = slab[pl.ds(a, rows // d4, stride=d4), :]
            a4_ref[0, a, :, lanes] = by4[a].astype(BF16)
        for res in range(d16):
            a16_ref[0, res, :, lanes] = (
                by4[res % d4, pl.ds(res // d4, rows // d16, stride=d16 // d4), :].astype(BF16))


def _project_retention_vg(xn_ref, win_ref, rv_ref, rg_ref):
    rv_ref[...] = _dot(xn_ref[...], win_ref[:, 2 * RET_WIDTH:3 * RET_WIDTH]).astype(BF16)
    rg_ref[...] = _dot(xn_ref[...], win_ref[:, 3 * RET_WIDTH:4 * RET_WIDTH])


def _ffn_in(x2, g1, wg, wu, wd, gm, win, rotation, zeta_rows, batch):
    t, d = x2.shape
    d_ff = wg.shape[1]
    n_ret = 4 * RET_WIDTH
    n_att = 3 * ATT_WIDTH
    rows = FFN_ROWS
    seq = t // batch
    per_seq = seq // rows
    d4, d16 = DILATIONS[1], DILATIONS[2]
    n_tiles = t // rows
    ffn_tile = lambda i: jnp.minimum(i, n_tiles - 1)
    proj_tile = lambda i: jnp.maximum(i - 1, 0)
    regroup = lambda dil: pl.BlockSpec((1, dil, rows // dil, n_att),
                                       lambda i: (proj_tile(i) // per_seq, 0, proj_tile(i) % per_seq, 0))
    by_rows = lambda width: pl.BlockSpec((rows, width), lambda i: (proj_tile(i), 0))
    position = pl.BlockSpec((rows, 2 * RET_HEAD_DIM), lambda i: (proj_tile(i) % per_seq, 0))
    in_hbm = pl.BlockSpec(memory_space=pl.ANY)
    return pl.pallas_call(
        functools.partial(_ffn_in_kernel, n_tiles + 1),
        grid=(n_tiles + 1,),
        in_specs=[
            pl.BlockSpec((rows, d), lambda i: (ffn_tile(i), 0)),
            _resident((1, d)),
            in_hbm, in_hbm, in_hbm,
            _resident((1, d)),
            in_hbm,
            position,
            _resident(zeta_rows.shape),
        ],
        out_specs=([pl.BlockSpec((rows, d), lambda i: (ffn_tile(i), 0)), by_rows(4 * RET_WIDTH),
                    by_rows(RET_WIDTH), by_rows(n_att), regroup(d4), regroup(d16)]),
        out_shape=(
            [jax.ShapeDtypeStruct((t, d), F32),
             jax.ShapeDtypeStruct((t, 4 * RET_WIDTH), BF16)]
            + [jax.ShapeDtypeStruct((t, RET_WIDTH), F32),
               jax.ShapeDtypeStruct((t, n_att), BF16),
               jax.ShapeDtypeStruct((batch, d4, seq // d4, n_att), BF16),
               jax.ShapeDtypeStruct((batch, d16, seq // d16, n_att), BF16)]),
        scratch_shapes=[pltpu.VMEM((rows, d), BF16), pltpu.VMEM((rows, d_ff), BF16),
                        pltpu.VMEM((n_att // LANES, rows, LANES), F32),
                        pltpu.VMEM((n_att // LANES, d4, rows // d4, LANES), F32),
                        pltpu.VMEM((2, rows, d), BF16)] + _weight_scratch(wg, wu, wd, win),
        compiler_params=pltpu.CompilerParams(
            dimension_semantics=("arbitrary",), vmem_limit_bytes=VMEM_LIMIT_BYTES),
        name="ffn_in",
    )(x2, g1, wg, wu, wd, gm, win, rotation, zeta_rows)


def _retention_tables(seq, tile_rows):
    d = RET_HEAD_DIM
    pos = jnp.arange(seq, dtype=F32)
    inv_freq = ROPE_BASE ** (-jnp.arange(0, d, 2, dtype=F32) / d)
    ang = pos[:, None] * jnp.repeat(inv_freq, 2)[None, :]
    even = (jnp.arange(d) % 2) == 0
    sin = jnp.sin(ang)
    rotation = jnp.concatenate([jnp.cos(ang), jnp.where(even[None, :], -sin, sin)], axis=1)

    c = RET_CHUNK
    log_g = jnp.log(1.0 - 2.0 ** (-5.0 - jnp.arange(RET_HEADS, dtype=F32)))
    idx = jnp.arange(c, dtype=F32)
    rel = idx[:, None] - idx[None, :]
    decay_in = jnp.where(rel >= 0, jnp.exp(log_g[:, None, None] * jnp.maximum(rel, 0.0)), 0.0)
    zeta = jnp.exp(log_g[:, None] * (c - 1 - idx)[None, :])
    xi = jnp.exp(log_g[:, None] * (idx + 1)[None, :])
    chunk_decay = jnp.exp(log_g * c)
    xi_lanes = jnp.broadcast_to(xi[:, :, None], (RET_HEADS, c, d))
    gamma = jnp.broadcast_to(chunk_decay[:, None, None], (RET_HEADS, 8, d))
    zeta_rows = jnp.tile(jnp.repeat(zeta.T, d, axis=1), (tile_rows // c, 1))
    return rotation, zeta_rows, decay_in, xi_lanes, gamma


def _retention_kernel(q_ref, k_ref, kz_ref, v_ref, dec_ref, xi_ref, gamma_ref, o_ref, state_ref):
    c = RET_CHUNK
    d = RET_HEAD_DIM

    @pl.when(pl.program_id(1) == 0)
    def _():
        state_ref[...] = jnp.zeros_like(state_ref)

    def chunk(j, carry):
        rows = pl.ds(pl.multiple_of(j * c, c), c)
        for h in range(RET_HEADS):
            lanes = slice(h * d, (h + 1) * d)
            qb = q_ref[0, rows, lanes]
            vb = v_ref[0, rows, lanes]
            scores = _dot_nt(qb, k_ref[0, rows, lanes]) * dec_ref[h]
            state = state_ref[h]
            o_ref[0, rows, lanes] = _dot(scores.astype(BF16), vb) + _dot(qb, state.astype(BF16)) * xi_ref[h]
            state_ref[h] = state * gamma_ref[h, 0:1, :] + _dot_tn(kz_ref[0, rows, lanes], vb)
        return carry

    lax.fori_loop(0, q_ref.shape[1] // c, chunk, 0, unroll=True)


def _retention(r, decay_in, xi_lanes, gamma):
    b, s, _ = r.shape
    w = RET_WIDTH
    rows = RET_ROWS
    d = RET_HEAD_DIM
    tile = lambda j: pl.BlockSpec((1, rows, w), lambda bi, si: (bi, si, j))
    return pl.pallas_call(
        _retention_kernel,
        grid=(b, s // rows),
        in_specs=[tile(0), tile(1), tile(2), tile(3),
                  _resident(decay_in.shape), _resident(xi_lanes.shape), _resident(gamma.shape)],
        out_specs=tile(0),
        out_shape=jax.ShapeDtypeStruct((b, s, w), F32),
        scratch_shapes=[pltpu.VMEM((RET_HEADS, d, d), F32)],
        compiler_params=pltpu.CompilerParams(
            dimension_semantics=("arbitrary", "arbitrary"), vmem_limit_bytes=VMEM_LIMIT_BYTES),
        name="retention",
    )(r, r, r, r, decay_in, xi_lanes, gamma)


ATT_BLOCKS = ATT_ROWS // BAND_BLOCK
KIND_ACC, KIND_M, KIND_L = 0, 1, 2
N_KINDS = 3


def _attention_bias():
    blk = BAND_BLOCK
    qi = np.arange(blk)[:, None]
    kj = np.arange(2 * blk)[None, :]
    dist = qi + blk - kj
    band = (dist >= 0) & (dist <= ATT_SPAN)
    masks = np.stack([band, band & (kj >= blk)])
    return jnp.asarray(np.where(masks, 0.0, -np.inf).astype(np.float32))


def _dilated_kernel(tiles_per_seq, n_steps, *refs):
    res_ref, y_ref = refs[-2:]
    vbufs = refs[-7:-4]
    step = pl.program_id(0)
    tile = step % tiles_per_seq

    @pl.when(step == 0)
    def _():
        for vbuf in vbufs:
            vbuf[:, LANES:2 * LANES] = jnp.ones((vbuf.shape[0], LANES), BF16)
        res_ref[1] = jnp.ones(res_ref.shape[1:], res_ref.dtype)
        y_ref[1] = jnp.ones(y_ref.shape[1:], y_ref.dtype)

    _run_lagged(lambda slot, lead: _dilated_body(slot, lead, tile, *refs), n_steps)


def _dilated_body(slot, lead, tile, q1, k1c, k1p, v1c, v1p, q4, k4c, k4p, v4c, v4p, q16, k16c, k16p, v16c, v16p,
                  bias_ref, o_ref, kb0, kb1, kb2, vb0, vb1, vb2, s_ref, m_ref, res_ref, y_ref):
    rows = ATT_ROWS
    blk = BAND_BLOCK
    hd = ATT_HEAD_DIM
    n_br = len(DILATIONS)
    kbufs, vbufs = (kb0, kb1, kb2), (vb0, vb1, vb2)
    d4, d16 = DILATIONS[1], DILATIONS[2]
    res_new, y_new = res_ref.at[slot], y_ref.at[slot]
    res_old, y_old = res_ref.at[1 - slot], y_ref.at[1 - slot]

    per_branch = (
        ([k1p.at[0]], [k1c.at[0]], [v1p.at[0]], [v1c.at[0]]),
        ([k4p.at[0, r] for r in range(d4)], [k4c.at[0, r] for r in range(d4)],
         [v4p.at[0, r] for r in range(d4)], [v4c.at[0, r] for r in range(d4)]),
        ([k16p.at[0, r] for r in range(d16)], [k16c.at[0, r] for r in range(d16)],
         [v16p.at[0, r] for r in range(d16)], [v16c.at[0, r] for r in range(d16)]),
    )
    for br, (kps, kcs, vps, vcs) in enumerate(per_branch if lead else ()):
        cur = rows // len(kps)
        for r in range(len(kps)):
            base = r * (blk + cur)
            kbufs[br][base:base + blk, :] = kps[r][...]
            kbufs[br][base + blk:base + blk + cur, :] = kcs[r][...]
            vbufs[br][base:base + blk, 0:LANES] = vps[r][...]
            vbufs[br][base + blk:base + blk + cur, 0:LANES] = vcs[r][...]

    head0 = lax.broadcasted_iota(jnp.int32, (blk, LANES), 1) < hd

    per_res = (ATT_BLOCKS, ATT_BLOCKS // d4, ATT_BLOCKS // d16)
    residue = lambda br, idx: idx // per_res[br]
    block_in_residue = lambda br, idx: idx % per_res[br]
    key_rows = lambda br, idx: slice((idx + residue(br, idx)) * blk, (idx + residue(br, idx) + 2) * blk)

    def load_q(br, idx):
        res, jb = residue(br, idx), block_in_residue(br, idx)
        if br == 0:
            return q1[0, jb * blk:(jb + 1) * blk, :]
        return (q4, q16)[br - 1][0, res, jb * blk:(jb + 1) * blk, :]

    def store_result(br, idx, kind, val):
        res, jb = residue(br, idx), block_in_residue(br, idx)
        if br == 0:
            res_new[kind, idx * blk:(idx + 1) * blk, :] = val
        elif br == 1:
            res_new[N_KINDS + kind, pl.ds(res + d4 * blk * jb, blk, stride=d4), :] = val
        else:
            y_new[kind, pl.ds((res % d4) * (rows // d4) + res // d4, blk, stride=d4), :] = val

    def pair(lo, hi):
        return jnp.where(head0, lo, hi)

    def scores(idx):
        slot = idx % 2
        for br in range(n_br):
            q = load_q(br, idx)
            zero = jnp.zeros_like(q)
            q2 = jnp.concatenate([jnp.where(head0, q, zero), jnp.where(head0, zero, q)], axis=0)
            s = _dot_nt(q2, kbufs[br][key_rows(br, idx), :])
            first = (tile == 0).astype(jnp.int32) if block_in_residue(br, idx) == 0 else 0
            bias = bias_ref[first]
            maxima = []
            for h in range(2):
                r = slice(h * blk, (h + 1) * blk)
                sh = s[r] + bias
                s_ref[slot, br, r, :] = sh
                m = jnp.broadcast_to(jnp.max(sh, axis=-1, keepdims=True), (blk, LANES))
                m_ref[slot, br, h] = m
                maxima.append(m)
            store_result(br, idx, KIND_M, pair(*maxima))

    def values(idx):
        slot = idx % 2
        for br in range(n_br):
            parts = []
            for h in range(2):
                m = m_ref[slot, br, h]
                s = s_ref[slot, br, h * blk:(h + 1) * blk, :]
                parts.append(jnp.exp(s - jnp.concatenate([m, m], axis=1)).astype(BF16))
            pv = _dot(jnp.concatenate(parts, axis=0), vbufs[br][key_rows(br, idx), :])
            store_result(br, idx, KIND_ACC, pair(pv[0:blk, 0:LANES], pv[blk:2 * blk, 0:LANES]))
            store_result(br, idx, KIND_L, pair(pv[0:blk, LANES:2 * LANES], pv[blk:2 * blk, LANES:2 * LANES]))

    sub = rows // d4

    def regroup_old(c):
        for kind in range(N_KINDS):
            for a in range(d4):
                res_old[2 * N_KINDS + kind, pl.ds(a + d4 * blk * c, blk, stride=d4), :] = (
                    y_old[kind, a * sub + c * blk:a * sub + (c + 1) * blk, :])

    def combine_old(i):
        r = slice(i * blk, (i + 1) * blk)
        ms = [res_old[br * N_KINDS + KIND_M, r, :] for br in range(n_br)]
        mx = jnp.maximum(jnp.maximum(ms[0], ms[1]), ms[2])
        ws = [jnp.exp(m - mx) for m in ms]
        num = sum(w * res_old[br * N_KINDS + KIND_ACC, r, :] for br, w in enumerate(ws))
        den = sum(w * res_old[br * N_KINDS + KIND_L, r, :] for br, w in enumerate(ws))
        o_ref[0, r, :] = (num / den).astype(o_ref.dtype)

    if lead:
        scores(0)
        scores(1)
        for i in range(ATT_BLOCKS):
            values(i)
            if i + 2 < ATT_BLOCKS:
                scores(i + 2)
    for i in range(ATT_BLOCKS):
        if i % d4 == 0:
            regroup_old(i // d4)
        combine_old(i)


def _dilated(a1, a4, a16, bias):
    b, s, _ = a1.shape
    rows = ATT_ROWS
    blk = BAND_BLOCK
    pairs = ATT_WIDTH // LANES
    d4, d16 = DILATIONS[1], DILATIONS[2]
    tiles = s // rows
    n_steps = b * pairs * tiles
    decode = lambda i: (i // (pairs * tiles), (i // tiles) % pairs, i % tiles)
    of_compute = lambda f: (lambda i: f(*decode(jnp.minimum(i, n_steps - 1))))
    of_combine = lambda f: (lambda i: f(*decode(jnp.maximum(i - 1, 0))))
    prev_of = lambda n: (lambda i: jnp.maximum(i * n - 1, 0))

    def specs(dil):
        sub = rows // dil
        per = sub // blk
        if dil == 1:
            cur = lambda off: pl.BlockSpec((1, sub, LANES), of_compute(lambda bi, pi, ti: (bi, ti, off + pi)))
            prv = lambda off: pl.BlockSpec(
                (1, blk, LANES), of_compute(lambda bi, pi, ti: (bi, prev_of(per)(ti), off + pi)))
        else:
            cur = lambda off: pl.BlockSpec(
                (1, dil, sub, LANES), of_compute(lambda bi, pi, ti: (bi, 0, ti, off + pi)))
            prv = lambda off: pl.BlockSpec(
                (1, dil, blk, LANES), of_compute(lambda bi, pi, ti: (bi, 0, prev_of(per)(ti), off + pi)))
        return [cur(0), cur(pairs), prv(pairs), cur(2 * pairs), prv(2 * pairs)]

    key_rows = lambda dil: (ATT_BLOCKS + dil) * blk
    return pl.pallas_call(
        functools.partial(_dilated_kernel, tiles, n_steps + 1),
        grid=(n_steps + 1,),
        in_specs=specs(1) + specs(d4) + specs(d16) + [_resident(bias.shape)],
        out_specs=pl.BlockSpec((1, rows, LANES), of_combine(lambda bi, pi, ti: (bi, ti, pi))),
        out_shape=jax.ShapeDtypeStruct((b, s, ATT_WIDTH), BF16),
        scratch_shapes=(
            [pltpu.VMEM((key_rows(dil), LANES), BF16) for dil in DILATIONS]
            + [pltpu.VMEM((key_rows(dil), 2 * LANES), BF16) for dil in DILATIONS]
            + [pltpu.VMEM((2, len(DILATIONS), 2 * blk, 2 * blk), F32),
               pltpu.VMEM((2, len(DILATIONS), 2, blk, LANES), F32),
               pltpu.VMEM((2, len(DILATIONS) * N_KINDS, rows, LANES), F32),
               pltpu.VMEM((2, N_KINDS, rows, LANES), F32)]),
        compiler_params=pltpu.CompilerParams(
            dimension_semantics=("arbitrary",), vmem_limit_bytes=VMEM_LIMIT_BYTES),
        name="dilated",
    )(a1, a1, a1, a1, a1, a4, a4, a4, a4, a4, a16, a16, a16, a16, a16, bias)


def _out_ffn_kernel(h_ref, ret_ref, gate_ref, att_ref, gain_ref, wo_hbm, g2_ref, wg_hbm, wu_hbm, wd_hbm, gf_ref,
                    y_ref, xn_ref, act_ref, mix_ref, h2_ref, wo_ref, wg_ref, wu_ref, wd_ref, stage_ref, sem,
                    *, final_norm, n_steps):
    d = RET_HEAD_DIM

    @pl.when(pl.program_id(0) == 0)
    def _():
        _load_weights_bf16([(wo_hbm, wo_ref), (wg_hbm, wg_ref), (wu_hbm, wu_ref), (wd_hbm, wd_ref)],
                           stage_ref, sem)
        h2_ref[1] = jnp.zeros(h2_ref.shape[1:], h2_ref.dtype)

    def body(slot, lead):
        if lead:
            h2_ref[slot] = h_ref[...] + _dot(att_ref[...], wo_ref[RET_WIDTH:, :])
        h_prev = h2_ref[1 - slot]
        xn_ref[...] = _rms_norm(h_prev, g2_ref[...]).astype(BF16)
        if lead:
            for hd in range(RET_HEADS):
                lanes = slice(hd * d, (hd + 1) * d)
                ret = ret_ref[:, lanes]
                cen = ret - jnp.mean(ret, axis=-1, keepdims=True)
                var = jnp.mean(cen * cen, axis=-1, keepdims=True)
                y = cen * lax.rsqrt(var + GN_EPS) * gain_ref[:, lanes]
                gate = gate_ref[:, lanes]
                mix_ref[:, lanes] = (y * (gate * jax.nn.sigmoid(gate))).astype(BF16)
        out = h2_ref[1 - slot] + 0.5 * _swiglu_half_step(xn_ref, wg_ref, wu_ref, wd_ref, act_ref)
        if lead:
            h2_ref[slot] = h2_ref[slot] + _dot(mix_ref[...], wo_ref[0:RET_WIDTH, :])
        if final_norm:
            out = _rms_norm(out, gf_ref[...])
        y_ref[...] = out

    _run_lagged(body, n_steps)


def _out_ffn(h1, ret, gate, att, gain, wo, g2, wg, wu, wd, gf, final_norm):
    t, d = h1.shape
    d_ff = wg.shape[1]
    rows = FFN_ROWS
    n_tiles = t // rows
    by_rows = lambda a: pl.BlockSpec((rows, a.shape[1]), lambda i: (jnp.minimum(i, n_tiles - 1), 0))
    in_hbm = pl.BlockSpec(memory_space=pl.ANY)
    return pl.pallas_call(
        functools.partial(_out_ffn_kernel, final_norm=final_norm, n_steps=n_tiles + 1),
        grid=(n_tiles + 1,),
        in_specs=[
            by_rows(h1), by_rows(ret), by_rows(gate), by_rows(att),
            _resident(gain.shape),
            in_hbm,
            _resident((1, d)),
            in_hbm, in_hbm, in_hbm,
            _resident((1, d)),
        ],
        out_specs=pl.BlockSpec((rows, d), lambda i: (jnp.maximum(i - 1, 0), 0)),
        out_shape=jax.ShapeDtypeStruct((t, d), F32),
        scratch_shapes=[pltpu.VMEM((rows, d), BF16), pltpu.VMEM((rows, d_ff), BF16),
                        pltpu.VMEM((rows, RET_WIDTH), BF16), pltpu.VMEM((2, rows, d), F32)]
        + _weight_scratch(wo, wg, wu, wd),
        compiler_params=pltpu.CompilerParams(
            dimension_semantics=("arbitrary",), vmem_limit_bytes=VMEM_LIMIT_BYTES),
        name="out_ffn",
    )(h1, ret, gate, att, gain, wo, g2, wg, wu, wd, gf)


def kernel(x, norm_ffn1, ffn1_w_gate, ffn1_w_up, ffn1_w_down, norm_mix, w_in, ret_norm_gain,
           w_out, norm_ffn2, ffn2_w_gate, ffn2_w_up, ffn2_w_down, norm_final):
    b, s, d = x.shape
    depth = norm_ffn1.shape[0]
    assert s % ATT_ROWS == 0 and s % RET_ROWS == 0 and (b * s) % FFN_ROWS == 0
    assert w_in.shape[2] == 4 * RET_WIDTH + 3 * ATT_WIDTH
    assert FFN_ROWS % RET_CHUNK == 0 and RET_ROWS % RET_CHUNK == 0
    rotation, zeta_rows, decay_in, xi_lanes, gamma = _retention_tables(s, FFN_ROWS)
    bias = _attention_bias()
    row = lambda v: v.reshape(1, -1)
    seq = lambda a: a.reshape(b, s, -1)

    h = x.reshape(b * s, d)
    for l in range(depth):
        h1, r, rg, a1, a4, a16 = _ffn_in(
            h, row(norm_ffn1[l]), ffn1_w_gate[l], ffn1_w_up[l], ffn1_w_down[l],
            row(norm_mix[l]), w_in[l], rotation, zeta_rows, b)
        ret = _retention(seq(r), decay_in, xi_lanes, gamma)
        att = _dilated(seq(a1), a4, a16, bias)
        h = _out_ffn(h1, ret.reshape(b * s, -1), rg, att.reshape(b * s, -1), row(ret_norm_gain[l]),
                     w_out[l], row(norm_ffn2[l]), ffn2_w_gate[l], ffn2_w_up[l], ffn2_w_down[l],
                     row(norm_final), final_norm=(l == depth - 1))
    return h.reshape(b, s, d)
```

```python
import functools

import jax
import jax.numpy as jnp
import numpy as np
from jax import lax
from jax.experimental import pallas as pl
from jax.experimental.pallas import tpu as pltpu

RET_HEADS = 4
RET_HEAD_DIM = 128
RET_WIDTH = RET_HEADS * RET_HEAD_DIM
ATT_HEADS = 8
ATT_HEAD_DIM = 64
ATT_WIDTH = ATT_HEADS * ATT_HEAD_DIM
DILATIONS = (1, 4, 16)
ATT_SPAN = 128
BAND_BLOCK = 128
ROPE_BASE = 10000.0
NORM_EPS = 1e-6
GN_EPS = 1e-6

LANES = 128
MXU_DIM = 256
VMEM_LIMIT_BYTES = 56 * 1024 * 1024
WEIGHT_CHUNK_ROWS = 128
WEIGHT_STAGES = 4

FFN_ROWS = 256
FF_CHUNK = MXU_DIM
RET_ROWS = 2048
RET_CHUNK = MXU_DIM
ATT_ROWS = ATT_SPAN * DILATIONS[-1]

F32 = jnp.float32
BF16 = jnp.bfloat16


def _dot(a, b):
    return jnp.dot(a, b, preferred_element_type=F32)


def _dot_nt(a, b):
    return lax.dot_general(a, b, (((1,), (1,)), ((), ())), preferred_element_type=F32)


def _dot_tn(a, b):
    return lax.dot_general(a, b, (((0,), (0,)), ((), ())), preferred_element_type=F32)


def _rms_norm(x, gain):
    return x * lax.rsqrt(jnp.mean(x * x, axis=-1, keepdims=True) + NORM_EPS) * gain


def _swiglu_half_step(xn_ref, wg_ref, wu_ref, wd_ref, act_ref, rows=slice(None)):
    d_ff = wg_ref.shape[1]
    for c in range(d_ff // FF_CHUNK):
        cols = slice(c * FF_CHUNK, (c + 1) * FF_CHUNK)
        g = _dot(xn_ref[rows, :], wg_ref[:, cols])
        u = _dot(xn_ref[rows, :], wu_ref[:, cols])
        act_ref[rows, cols] = (g * jax.nn.sigmoid(g) * u).astype(BF16)
    return _dot(act_ref[rows, :], wd_ref[...])


def _run_lagged(body, n_steps):
    step = pl.program_id(0)
    last = n_steps - 1
    for slot in range(2):
        pl.when((jnp.bitwise_and(step, 1) == slot) & (step < last))(functools.partial(body, slot, True))
    pl.when(step == last)(functools.partial(body, last % 2, False))


def _load_weights_bf16(pairs, stage_ref, sem):
    rows = WEIGHT_CHUNK_ROWS
    depth = stage_ref.shape[0]
    chunks = [(src, dst, c) for src, dst in pairs for c in range(src.shape[0] // rows)]

    def copy(k):
        src, _, c = chunks[k]
        return pltpu.make_async_copy(src.at[c * rows:(c + 1) * rows, :],
                                     stage_ref.at[k % depth, :, 0:src.shape[1]], sem.at[k % depth])

    for k in range(min(depth - 1, len(chunks))):
        copy(k).start()
    for k, (src, dst, c) in enumerate(chunks):
        if k + depth - 1 < len(chunks):
            copy(k + depth - 1).start()
        copy(k).wait()
        dst[c * rows:(c + 1) * rows, :] = stage_ref[k % depth, :, 0:src.shape[1]].astype(BF16)


def _weight_scratch(*weights):
    widest = max(w.shape[1] for w in weights)
    assert all(w.dtype == F32 and w.shape[0] % WEIGHT_CHUNK_ROWS == 0 for w in weights)
    return ([pltpu.VMEM(w.shape, BF16) for w in weights]
            + [pltpu.VMEM((WEIGHT_STAGES, WEIGHT_CHUNK_ROWS, widest), F32),
               pltpu.SemaphoreType.DMA((WEIGHT_STAGES,))])


def _resident(shape):
    return pl.BlockSpec(shape, lambda *_: (0,) * len(shape), pipeline_mode=pl.Buffered(1))


def _rotary(t, cos, sin_signed, even_lane):
    d = t.shape[-1]
    swapped = jnp.where(even_lane, pltpu.roll(t, d - 1, 1), pltpu.roll(t, 1, 1))
    return t * cos + swapped * sin_signed


def _ffn_in_kernel(n_steps, x_ref, g1_ref, wg_hbm, wu_hbm, wd_hbm, gm_ref, win_hbm, rot_ref, zeta_ref,
                   h_ref, r_ref, rg_ref, a1_ref, a4_ref, a16_ref,
                   xn_ref, act_ref, slab_ref, by4_ref, hn_ref, wg_ref, wu_ref, wd_ref, win_ref, stage_ref, sem):
    @pl.when(pl.program_id(0) == 0)
    def _():
        _load_weights_bf16([(wg_hbm, wg_ref), (wu_hbm, wu_ref), (wd_hbm, wd_ref), (win_hbm, win_ref)],
                           stage_ref, sem)
        hn_ref[1] = jnp.zeros(hn_ref.shape[1:], hn_ref.dtype)

    rq_ref, rk_ref, rkz_ref, rv_ref = (r_ref.at[:, j * RET_WIDTH:(j + 1) * RET_WIDTH] for j in range(4))

    def body(slot, lead):
        prev = hn_ref.at[1 - slot]
        _project_attention(prev, win_ref, a1_ref, slab_ref)
        _regroup_attention(slab_ref, by4_ref, a4_ref, a16_ref)
        if lead:
            x = x_ref[...]
            xn_ref[...] = _rms_norm(x, g1_ref[...]).astype(BF16)
            h = x + 0.5 * _swiglu_half_step(xn_ref, wg_ref, wu_ref, wd_ref, act_ref)
            h_ref[...] = h
            hn_ref[slot] = _rms_norm(h, gm_ref[...]).astype(BF16)
        _project_retention_qk(prev, win_ref, rot_ref, zeta_ref, rq_ref, rk_ref, rkz_ref)
        _project_retention_vg(prev, win_ref, rv_ref, rg_ref)

    _run_lagged(body, n_steps)


def _project_retention_qk(xn_ref, win_ref, rot_ref, zeta_ref, rq_ref, rk_ref, rkz_ref):
    rows = xn_ref.shape[0]
    hd = RET_HEAD_DIM
    even_lane = (lax.broadcasted_iota(jnp.int32, (rows, hd), 1) % 2) == 0
    cos = rot_ref[:, 0:hd]
    sin = rot_ref[:, hd:2 * hd]
    for c in range(RET_WIDTH // MXU_DIM):
        cols = slice(c * MXU_DIM, (c + 1) * MXU_DIM)
        uq = _dot(xn_ref[...], win_ref[:, cols])
        uk = _dot(xn_ref[...], win_ref[:, RET_WIDTH + c * MXU_DIM:RET_WIDTH + (c + 1) * MXU_DIM])
        for half in range(MXU_DIM // hd):
            lanes = slice(c * MXU_DIM + half * hd, c * MXU_DIM + (half + 1) * hd)
            part = slice(half * hd, (half + 1) * hd)
            rq_ref[:, lanes] = _rotary(uq[:, part], cos, sin, even_lane).astype(BF16)
            k = _rotary(uk[:, part], cos, sin, even_lane) * (RET_HEAD_DIM ** -0.5)
            rk_ref[:, lanes] = k.astype(BF16)
            rkz_ref[:, lanes] = (k * zeta_ref[:, lanes]).astype(BF16)


def _project_attention(xn_ref, win_ref, a1_ref, slab_ref):
    n_ret = 4 * RET_WIDTH
    for c in range(slab_ref.shape[0]):
        cols = slice(c * MXU_DIM, (c + 1) * MXU_DIM)
        u = _dot(xn_ref[...], win_ref[:, n_ret + c * MXU_DIM:n_ret + (c + 1) * MXU_DIM])
        if (c + 1) * MXU_DIM <= ATT_WIDTH:
            u = u * (ATT_HEAD_DIM ** -0.5)
        ub = u.astype(BF16)
        a1_ref[:, cols] = ub
        rounded = ub.astype(F32)
        slab_ref[c] = pltpu.pack_elementwise([rounded[:, 0:LANES], rounded[:, LANES:2 * LANES]],
                                             packed_dtype=BF16)


def _unpack_bf16_pair(words):
    return [pltpu.unpack_elementwise(words, index=i, packed_dtype=BF16, unpacked_dtype=F32).astype(BF16)
            for i in range(2)]


def _regroup_attention(slab_ref, by4_ref, a4_ref, a16_ref):
    rows = slab_ref.shape[1]
    d4, d16 = DILATIONS[1], DILATIONS[2]
    for s in range(slab_ref.shape[0]):
        lanes = [slice(s * MXU_DIM + h * LANES, s * MXU_DIM + (h + 1) * LANES) for h in range(2)]
        slab = slab_ref.at[s]
        by4 = by4_ref.at[s]
        for a in range(d4):
            words = slab[pl.ds(a, rows // d4, stride=d4), :]
            by4[a] = words
            for lane, part in zip(lanes, _unpack_bf16_pair(words)):
                a4_ref[0, a, :, lane] = part
        for res in range(d16):
            words = by4[res % d4, pl.ds(res // d4, rows // d16, stride=d16 // d4), :]
            for lane, part in zip(lanes, _unpack_bf16_pair(words)):
                a16_ref[0, res, :, lane] = part


def _project_retention_vg(xn_ref, win_ref, rv_ref, rg_ref):
    rv_ref[...] = _dot(xn_ref[...], win_ref[:, 2 * RET_WIDTH:3 * RET_WIDTH]).astype(BF16)
    rg_ref[...] = _dot(xn_ref[...], win_ref[:, 3 * RET_WIDTH:4 * RET_WIDTH])


def _ffn_in(x2, g1, wg, wu, wd, gm, win, rotation, zeta_rows, batch):
    t, d = x2.shape
    d_ff = wg.shape[1]
    n_ret = 4 * RET_WIDTH
    n_att = 3 * ATT_WIDTH
    rows = FFN_ROWS
    seq = t // batch
    per_seq = seq // rows
    d4, d16 = DILATIONS[1], DILATIONS[2]
    n_tiles = t // rows
    ffn_tile = lambda i: jnp.minimum(i, n_tiles - 1)
    proj_tile = lambda i: jnp.maximum(i - 1, 0)
    regroup = lambda dil: pl.BlockSpec((1, dil, rows // dil, n_att),
                                       lambda i: (proj_tile(i) // per_seq, 0, proj_tile(i) % per_seq, 0))
    by_rows = lambda width: pl.BlockSpec((rows, width), lambda i: (proj_tile(i), 0))
    position = pl.BlockSpec((rows, 2 * RET_HEAD_DIM), lambda i: (proj_tile(i) % per_seq, 0))
    in_hbm = pl.BlockSpec(memory_space=pl.ANY)
    return pl.pallas_call(
        functools.partial(_ffn_in_kernel, n_tiles + 1),
        grid=(n_tiles + 1,),
        in_specs=[
            pl.BlockSpec((rows, d), lambda i: (ffn_tile(i), 0)),
            _resident((1, d)),
            in_hbm, in_hbm, in_hbm,
            _resident((1, d)),
            in_hbm,
            position,
            _resident(zeta_rows.shape),
        ],
        out_specs=([pl.BlockSpec((rows, d), lambda i: (ffn_tile(i), 0)), by_rows(4 * RET_WIDTH),
                    by_rows(RET_WIDTH), by_rows(n_att), regroup(d4), regroup(d16)]),
        out_shape=(
            [jax.ShapeDtypeStruct((t, d), F32),
             jax.ShapeDtypeStruct((t, 4 * RET_WIDTH), BF16)]
            + [jax.ShapeDtypeStruct((t, RET_WIDTH), F32),
               jax.ShapeDtypeStruct((t, n_att), BF16),
               jax.ShapeDtypeStruct((batch, d4, seq // d4, n_att), BF16),
               jax.ShapeDtypeStruct((batch, d16, seq // d16, n_att), BF16)]),
        scratch_shapes=[pltpu.VMEM((rows, d), BF16), pltpu.VMEM((rows, d_ff), BF16),
                        pltpu.VMEM((n_att // MXU_DIM, rows, LANES), jnp.uint32),
                        pltpu.VMEM((n_att // MXU_DIM, d4, rows // d4, LANES), jnp.uint32),
                        pltpu.VMEM((2, rows, d), BF16)] + _weight_scratch(wg, wu, wd, win),
        compiler_params=pltpu.CompilerParams(
            dimension_semantics=("arbitrary",), vmem_limit_bytes=VMEM_LIMIT_BYTES),
        name="ffn_in",
    )(x2, g1, wg, wu, wd, gm, win, rotation, zeta_rows)


def _retention_tables(seq, tile_rows):
    d = RET_HEAD_DIM
    pos = jnp.arange(seq, dtype=F32)
    inv_freq = ROPE_BASE ** (-jnp.arange(0, d, 2, dtype=F32) / d)
    ang = pos[:, None] * jnp.repeat(inv_freq, 2)[None, :]
    even = (jnp.arange(d) % 2) == 0
    sin = jnp.sin(ang)
    rotation = jnp.concatenate([jnp.cos(ang), jnp.where(even[None, :], -sin, sin)], axis=1)

    c = RET_CHUNK
    log_g = jnp.log(1.0 - 2.0 ** (-5.0 - jnp.arange(RET_HEADS, dtype=F32)))
    idx = jnp.arange(c, dtype=F32)
    rel = idx[:, None] - idx[None, :]
    decay_in = jnp.where(rel >= 0, jnp.exp(log_g[:, None, None] * jnp.maximum(rel, 0.0)), 0.0)
    zeta = jnp.exp(log_g[:, None] * (c - 1 - idx)[None, :])
    xi = jnp.exp(log_g[:, None] * (idx + 1)[None, :])
    chunk_decay = jnp.exp(log_g * c)
    xi_lanes = jnp.broadcast_to(xi[:, :, None], (RET_HEADS, c, d))
    gamma = jnp.broadcast_to(chunk_decay[:, None, None], (RET_HEADS, 8, d))
    zeta_rows = jnp.tile(jnp.repeat(zeta.T, d, axis=1), (tile_rows // c, 1))
    return rotation, zeta_rows, decay_in, xi_lanes, gamma


def _retention_kernel(q_ref, k_ref, kz_ref, v_ref, dec_ref, xi_ref, gamma_ref, o_ref, state_ref):
    c = RET_CHUNK
    d = RET_HEAD_DIM

    @pl.when(pl.program_id(1) == 0)
    def _():
        state_ref[...] = jnp.zeros_like(state_ref)

    def chunk(j, carry):
        rows = pl.ds(pl.multiple_of(j * c, c), c)
        for h in range(RET_HEADS):
            lanes = slice(h * d, (h + 1) * d)
            qb = q_ref[0, rows, lanes]
            vb = v_ref[0, rows, lanes]
            scores = _dot_nt(qb, k_ref[0, rows, lanes]) * dec_ref[h]
            state = state_ref[h]
            o_ref[0, rows, lanes] = _dot(scores.astype(BF16), vb) + _dot(qb, state.astype(BF16)) * xi_ref[h]
            state_ref[h] = state * gamma_ref[h, 0:1, :] + _dot_tn(kz_ref[0, rows, lanes], vb)
        return carry

    lax.fori_loop(0, q_ref.shape[1] // c, chunk, 0, unroll=True)


def _retention(r, decay_in, xi_lanes, gamma):
    b, s, _ = r.shape
    w = RET_WIDTH
    rows = RET_ROWS
    d = RET_HEAD_DIM
    tile = lambda j: pl.BlockSpec((1, rows, w), lambda bi, si: (bi, si, j))
    return pl.pallas_call(
        _retention_kernel,
        grid=(b, s // rows),
        in_specs=[tile(0), tile(1), tile(2), tile(3),
                  _resident(decay_in.shape), _resident(xi_lanes.shape), _resident(gamma.shape)],
        out_specs=tile(0),
        out_shape=jax.ShapeDtypeStruct((b, s, w), F32),
        scratch_shapes=[pltpu.VMEM((RET_HEADS, d, d), F32)],
        compiler_params=pltpu.CompilerParams(
            dimension_semantics=("arbitrary", "arbitrary"), vmem_limit_bytes=VMEM_LIMIT_BYTES),
        name="retention",
    )(r, r, r, r, decay_in, xi_lanes, gamma)


ATT_BLOCKS = ATT_ROWS // BAND_BLOCK
KIND_ACC, KIND_M, KIND_L = 0, 1, 2
N_KINDS = 3


def _attention_bias():
    blk = BAND_BLOCK
    qi = np.arange(blk)[:, None]
    kj = np.arange(2 * blk)[None, :]
    dist = qi + blk - kj
    band = (dist >= 0) & (dist <= ATT_SPAN)
    masks = np.stack([band, band & (kj >= blk)])
    return jnp.asarray(np.where(masks, 0.0, -np.inf).astype(np.float32))


def _dilated_kernel(tiles_per_seq, n_steps, *refs):
    res_ref, y_ref = refs[-2:]
    vbufs = refs[-7:-4]
    step = pl.program_id(0)
    tile = step % tiles_per_seq

    @pl.when(step == 0)
    def _():
        for vbuf in vbufs:
            vbuf[:, LANES:2 * LANES] = jnp.ones((vbuf.shape[0], LANES), BF16)
        res_ref[1] = jnp.ones(res_ref.shape[1:], res_ref.dtype)
        y_ref[1] = jnp.ones(y_ref.shape[1:], y_ref.dtype)

    _run_lagged(lambda slot, lead: _dilated_body(slot, lead, tile, *refs), n_steps)


def _dilated_body(slot, lead, tile, q1, k1c, k1p, v1c, v1p, q4, k4c, k4p, v4c, v4p, q16, k16c, k16p, v16c, v16p,
                  bias_ref, o_ref, kb0, kb1, kb2, vb0, vb1, vb2, s_ref, m_ref, res_ref, y_ref):
    rows = ATT_ROWS
    blk = BAND_BLOCK
    hd = ATT_HEAD_DIM
    n_br = len(DILATIONS)
    kbufs, vbufs = (kb0, kb1, kb2), (vb0, vb1, vb2)
    d4, d16 = DILATIONS[1], DILATIONS[2]
    res_new, y_new = res_ref.at[slot], y_ref.at[slot]
    res_old, y_old = res_ref.at[1 - slot], y_ref.at[1 - slot]

    per_branch = (
        ([k1p.at[0]], [k1c.at[0]], [v1p.at[0]], [v1c.at[0]]),
        ([k4p.at[0, r] for r in range(d4)], [k4c.at[0, r] for r in range(d4)],
         [v4p.at[0, r] for r in range(d4)], [v4c.at[0, r] for r in range(d4)]),
        ([k16p.at[0, r] for r in range(d16)], [k16c.at[0, r] for r in range(d16)],
         [v16p.at[0, r] for r in range(d16)], [v16c.at[0, r] for r in range(d16)]),
    )
    for br, (kps, kcs, vps, vcs) in enumerate(per_branch if lead else ()):
        cur = rows // len(kps)
        for r in range(len(kps)):
            base = r * (blk + cur)
            kbufs[br][base:base + blk, :] = kps[r][...]
            kbufs[br][base + blk:base + blk + cur, :] = kcs[r][...]
            vbufs[br][base:base + blk, 0:LANES] = vps[r][...]
            vbufs[br][base + blk:base + blk + cur, 0:LANES] = vcs[r][...]

    head0 = lax.broadcasted_iota(jnp.int32, (blk, LANES), 1) < hd

    per_res = (ATT_BLOCKS, ATT_BLOCKS // d4, ATT_BLOCKS // d16)
    residue = lambda br, idx: idx // per_res[br]
    block_in_residue = lambda br, idx: idx % per_res[br]
    key_rows = lambda br, idx: slice((idx + residue(br, idx)) * blk, (idx + residue(br, idx) + 2) * blk)

    def load_q(br, idx):
        res, jb = residue(br, idx), block_in_residue(br, idx)
        if br == 0:
            return q1[0, jb * blk:(jb + 1) * blk, :]
        return (q4, q16)[br - 1][0, res, jb * blk:(jb + 1) * blk, :]

    def store_result(br, idx, kind, val):
        res, jb = residue(br, idx), block_in_residue(br, idx)
        if br == 0:
            res_new[kind, idx * blk:(idx + 1) * blk, :] = val
        elif br == 1:
            res_new[N_KINDS + kind, pl.ds(res + d4 * blk * jb, blk, stride=d4), :] = val
        else:
            y_new[kind, pl.ds((res % d4) * (rows // d4) + res // d4, blk, stride=d4), :] = val

    def pair(lo, hi):
        return jnp.where(head0, lo, hi)

    def scores(idx):
        slot = idx % 2
        for br in range(n_br):
            q = load_q(br, idx)
            zero = jnp.zeros_like(q)
            q2 = jnp.concatenate([jnp.where(head0, q, zero), jnp.where(head0, zero, q)], axis=0)
            s = _dot_nt(q2, kbufs[br][key_rows(br, idx), :])
            first = (tile == 0).astype(jnp.int32) if block_in_residue(br, idx) == 0 else 0
            bias = bias_ref[first]
            maxima = []
            for h in range(2):
                r = slice(h * blk, (h + 1) * blk)
                sh = s[r] + bias
                s_ref[slot, br, r, :] = sh
                m = jnp.broadcast_to(jnp.max(sh, axis=-1, keepdims=True), (blk, LANES))
                m_ref[slot, br, h] = m
                maxima.append(m)
            store_result(br, idx, KIND_M, pair(*maxima))

    def values(idx):
        slot = idx % 2
        for br in range(n_br):
            parts = []
            for h in range(2):
                m = m_ref[slot, br, h]
                s = s_ref[slot, br, h * blk:(h + 1) * blk, :]
                parts.append(jnp.exp(s - jnp.concatenate([m, m], axis=1)).astype(BF16))
            pv = _dot(jnp.concatenate(parts, axis=0), vbufs[br][key_rows(br, idx), :])
            store_result(br, idx, KIND_ACC, pair(pv[0:blk, 0:LANES], pv[blk:2 * blk, 0:LANES]))
            store_result(br, idx, KIND_L, pair(pv[0:blk, LANES:2 * LANES], pv[blk:2 * blk, LANES:2 * LANES]))

    sub = rows // d4

    def regroup_old(c):
        for kind in range(N_KINDS):
            for a in range(d4):
                res_old[2 * N_KINDS + kind, pl.ds(a + d4 * blk * c, blk, stride=d4), :] = (
                    y_old[kind, a * sub + c * blk:a * sub + (c + 1) * blk, :])

    def combine_old(i):
        r = slice(i * blk, (i + 1) * blk)
        ms = [res_old[br * N_KINDS + KIND_M, r, :] for br in range(n_br)]
        mx = jnp.maximum(jnp.maximum(ms[0], ms[1]), ms[2])
        ws = [jnp.exp(m - mx) for m in ms]
        num = sum(w * res_old[br * N_KINDS + KIND_ACC, r, :] for br, w in enumerate(ws))
        den = sum(w * res_old[br * N_KINDS + KIND_L, r, :] for br, w in enumerate(ws))
        o_ref[0, r, :] = (num / den).astype(o_ref.dtype)

    if lead:
        scores(0)
        scores(1)
        for i in range(ATT_BLOCKS):
            values(i)
            if i + 2 < ATT_BLOCKS:
                scores(i + 2)
    for i in range(ATT_BLOCKS):
        if i % d4 == 0:
            regroup_old(i // d4)
        combine_old(i)


def _dilated(a1, a4, a16, bias):
    b, s, _ = a1.shape
    rows = ATT_ROWS
    blk = BAND_BLOCK
    pairs = ATT_WIDTH // LANES
    d4, d16 = DILATIONS[1], DILATIONS[2]
    tiles = s // rows
    n_steps = b * pairs * tiles
    decode = lambda i: (i // (pairs * tiles), (i // tiles) % pairs, i % tiles)
    of_compute = lambda f: (lambda i: f(*decode(jnp.minimum(i, n_steps - 1))))
    of_combine = lambda f: (lambda i: f(*decode(jnp.maximum(i - 1, 0))))
    prev_of = lambda n: (lambda i: jnp.maximum(i * n - 1, 0))

    def specs(dil):
        sub = rows // dil
        per = sub // blk
        if dil == 1:
            cur = lambda off: pl.BlockSpec((1, sub, LANES), of_compute(lambda bi, pi, ti: (bi, ti, off + pi)))
            prv = lambda off: pl.BlockSpec(
                (1, blk, LANES), of_compute(lambda bi, pi, ti: (bi, prev_of(per)(ti), off + pi)))
        else:
            cur = lambda off: pl.BlockSpec(
                (1, dil, sub, LANES), of_compute(lambda bi, pi, ti: (bi, 0, ti, off + pi)))
            prv = lambda off: pl.BlockSpec(
                (1, dil, blk, LANES), of_compute(lambda bi, pi, ti: (bi, 0, prev_of(per)(ti), off + pi)))
        return [cur(0), cur(pairs), prv(pairs), cur(2 * pairs), prv(2 * pairs)]

    key_rows = lambda dil: (ATT_BLOCKS + dil) * blk
    return pl.pallas_call(
        functools.partial(_dilated_kernel, tiles, n_steps + 1),
        grid=(n_steps + 1,),
        in_specs=specs(1) + specs(d4) + specs(d16) + [_resident(bias.shape)],
        out_specs=pl.BlockSpec((1, rows, LANES), of_combine(lambda bi, pi, ti: (bi, ti, pi))),
        out_shape=jax.ShapeDtypeStruct((b, s, ATT_WIDTH), BF16),
        scratch_shapes=(
            [pltpu.VMEM((key_rows(dil), LANES), BF16) for dil in DILATIONS]
            + [pltpu.VMEM((key_rows(dil), 2 * LANES), BF16) for dil in DILATIONS]
            + [pltpu.VMEM((2, len(DILATIONS), 2 * blk, 2 * blk), F32),
               pltpu.VMEM((2, len(DILATIONS), 2, blk, LANES), F32),
               pltpu.VMEM((2, len(DILATIONS) * N_KINDS, rows, LANES), F32),
               pltpu.VMEM((2, N_KINDS, rows, LANES), F32)]),
        compiler_params=pltpu.CompilerParams(
            dimension_semantics=("arbitrary",), vmem_limit_bytes=VMEM_LIMIT_BYTES),
        name="dilated",
    )(a1, a1, a1, a1, a1, a4, a4, a4, a4, a4, a16, a16, a16, a16, a16, bias)


def _out_ffn_kernel(h_ref, ret_ref, gate_ref, att_ref, gain_ref, wo_hbm, g2_ref, wg_hbm, wu_hbm, wd_hbm, gf_ref,
                    y_ref, xn_ref, act_ref, mix_ref, h2_ref, wo_ref, wg_ref, wu_ref, wd_ref, stage_ref, sem,
                    *, final_norm, n_steps):
    d = RET_HEAD_DIM

    @pl.when(pl.program_id(0) == 0)
    def _():
        _load_weights_bf16([(wo_hbm, wo_ref), (wg_hbm, wg_ref), (wu_hbm, wu_ref), (wd_hbm, wd_ref)],
                           stage_ref, sem)
        h2_ref[1] = jnp.zeros(h2_ref.shape[1:], h2_ref.dtype)

    def body(slot, lead):
        if lead:
            h2_ref[slot] = h_ref[...] + _dot(att_ref[...], wo_ref[RET_WIDTH:, :])
        h_prev = h2_ref[1 - slot]
        xn_ref[...] = _rms_norm(h_prev, g2_ref[...]).astype(BF16)
        if lead:
            for hd in range(RET_HEADS):
                lanes = slice(hd * d, (hd + 1) * d)
                ret = ret_ref[:, lanes]
                cen = ret - jnp.mean(ret, axis=-1, keepdims=True)
                var = jnp.mean(cen * cen, axis=-1, keepdims=True)
                y = cen * lax.rsqrt(var + GN_EPS) * gain_ref[:, lanes]
                gate = gate_ref[:, lanes]
                mix_ref[:, lanes] = (y * (gate * jax.nn.sigmoid(gate))).astype(BF16)
        out = h2_ref[1 - slot] + 0.5 * _swiglu_half_step(xn_ref, wg_ref, wu_ref, wd_ref, act_ref)
        if lead:
            h2_ref[slot] = h2_ref[slot] + _dot(mix_ref[...], wo_ref[0:RET_WIDTH, :])
        if final_norm:
            out = _rms_norm(out, gf_ref[...])
        y_ref[...] = out

    _run_lagged(body, n_steps)


def _out_ffn(h1, ret, gate, att, gain, wo, g2, wg, wu, wd, gf, final_norm):
    t, d = h1.shape
    d_ff = wg.shape[1]
    rows = FFN_ROWS
    n_tiles = t // rows
    by_rows = lambda a: pl.BlockSpec((rows, a.shape[1]), lambda i: (jnp.minimum(i, n_tiles - 1), 0))
    in_hbm = pl.BlockSpec(memory_space=pl.ANY)
    return pl.pallas_call(
        functools.partial(_out_ffn_kernel, final_norm=final_norm, n_steps=n_tiles + 1),
        grid=(n_tiles + 1,),
        in_specs=[
            by_rows(h1), by_rows(ret), by_rows(gate), by_rows(att),
            _resident(gain.shape),
            in_hbm,
            _resident((1, d)),
            in_hbm, in_hbm, in_hbm,
            _resident((1, d)),
        ],
        out_specs=pl.BlockSpec((rows, d), lambda i: (jnp.maximum(i - 1, 0), 0)),
        out_shape=jax.ShapeDtypeStruct((t, d), F32),
        scratch_shapes=[pltpu.VMEM((rows, d), BF16), pltpu.VMEM((rows, d_ff), BF16),
                        pltpu.VMEM((rows, RET_WIDTH), BF16), pltpu.VMEM((2, rows, d), F32)]
        + _weight_scratch(wo, wg, wu, wd),
        compiler_params=pltpu.CompilerParams(
            dimension_semantics=("arbitrary",), vmem_limit_bytes=VMEM_LIMIT_BYTES),
        name="out_ffn",
    )(h1, ret, gate, att, gain, wo, g2, wg, wu, wd, gf)


def kernel(x, norm_ffn1, ffn1_w_gate, ffn1_w_up, ffn1_w_down, norm_mix, w_in, ret_norm_gain,
           w_out, norm_ffn2, ffn2_w_gate, ffn2_w_up, ffn2_w_down, norm_final):
    b, s, d = x.shape
    depth = norm_ffn1.shape[0]
    assert s % ATT_ROWS == 0 and s % RET_ROWS == 0 and (b * s) % FFN_ROWS == 0
    assert w_in.shape[2] == 4 * RET_WIDTH + 3 * ATT_WIDTH
    assert FFN_ROWS % RET_CHUNK == 0 and RET_ROWS % RET_CHUNK == 0
    rotation, zeta_rows, decay_in, xi_lanes, gamma = _retention_tables(s, FFN_ROWS)
    bias = _attention_bias()
    row = lambda v: v.reshape(1, -1)
    seq = lambda a: a.reshape(b, s, -1)

    h = x.reshape(b * s, d)
    for l in range(depth):
        h1, r, rg, a1, a4, a16 = _ffn_in(
            h, row(norm_ffn1[l]), ffn1_w_gate[l], ffn1_w_up[l], ffn1_w_down[l],
            row(norm_mix[l]), w_in[l], rotation, zeta_rows, b)
        ret = _retention(seq(r), decay_in, xi_lanes, gamma)
        att = _dilated(seq(a1), a4, a16, bias)
        h = _out_ffn(h1, ret.reshape(b * s, -1), rg, att.reshape(b * s, -1), row(ret_norm_gain[l]),
                     w_out[l], row(norm_ffn2[l]), ffn2_w_gate[l], ffn2_w_up[l], ffn2_w_down[l],
                     row(norm_final), final_norm=(l == depth - 1))
    return h.reshape(b, s, d)
```

```python
import functools

import jax
import jax.numpy as jnp
import numpy as np
from jax import lax
from jax.experimental import pallas as pl
from jax.experimental.pallas import tpu as pltpu

RET_HEADS = 4
RET_HEAD_DIM = 128
RET_WIDTH = RET_HEADS * RET_HEAD_DIM
ATT_HEADS = 8
ATT_HEAD_DIM = 64
ATT_WIDTH = ATT_HEADS * ATT_HEAD_DIM
DILATIONS = (1, 4, 16)
ATT_SPAN = 128
BAND_BLOCK = 128
ROPE_BASE = 10000.0
NORM_EPS = 1e-6
GN_EPS = 1e-6

LANES = 128
MXU_DIM = 256
VMEM_LIMIT_BYTES = 56 * 1024 * 1024
WEIGHT_CHUNK_ROWS = 128
WEIGHT_STAGES = 4

FFN_ROWS = 256
FF_CHUNK = MXU_DIM
RET_ROWS = 2048
RET_CHUNK = MXU_DIM
ATT_ROWS = ATT_SPAN * DILATIONS[-1]

F32 = jnp.float32
BF16 = jnp.bfloat16


def _dot(a, b):
    return jnp.dot(a, b, preferred_element_type=F32)


def _dot_nt(a, b):
    return lax.dot_general(a, b, (((1,), (1,)), ((), ())), preferred_element_type=F32)


def _dot_tn(a, b):
    return lax.dot_general(a, b, (((0,), (0,)), ((), ())), preferred_element_type=F32)


def _rms_norm(x, gain):
    return x * lax.rsqrt(jnp.mean(x * x, axis=-1, keepdims=True) + NORM_EPS) * gain


def _swiglu_half_step(xn_ref, wg_ref, wu_ref, wd_ref, act_ref, rows=slice(None)):
    d_ff = wg_ref.shape[1]
    for c in range(d_ff // FF_CHUNK):
        cols = slice(c * FF_CHUNK, (c + 1) * FF_CHUNK)
        g = _dot(xn_ref[rows, :], wg_ref[:, cols])
        u = _dot(xn_ref[rows, :], wu_ref[:, cols])
        act_ref[rows, cols] = (g * jax.nn.sigmoid(g) * u).astype(BF16)
    return _dot(act_ref[rows, :], wd_ref[...])


def _run_lagged(body, n_steps):
    step = pl.program_id(0)
    last = n_steps - 1
    for slot in range(2):
        pl.when((jnp.bitwise_and(step, 1) == slot) & (step < last))(functools.partial(body, slot, True))
    pl.when(step == last)(functools.partial(body, last % 2, False))


def _load_weights_bf16(pairs, stage_ref, sem):
    rows = WEIGHT_CHUNK_ROWS
    depth = stage_ref.shape[0]
    chunks = [(src, dst, c) for src, dst in pairs for c in range(src.shape[0] // rows)]

    def copy(k):
        src, _, c = chunks[k]
        return pltpu.make_async_copy(src.at[c * rows:(c + 1) * rows, :],
                                     stage_ref.at[k % depth, :, 0:src.shape[1]], sem.at[k % depth])

    for k in range(min(depth - 1, len(chunks))):
        copy(k).start()
    for k, (src, dst, c) in enumerate(chunks):
        if k + depth - 1 < len(chunks):
            copy(k + depth - 1).start()
        copy(k).wait()
        dst[c * rows:(c + 1) * rows, :] = stage_ref[k % depth, :, 0:src.shape[1]].astype(BF16)


def _weight_scratch(*weights):
    widest = max(w.shape[1] for w in weights)
    assert all(w.dtype == F32 and w.shape[0] % WEIGHT_CHUNK_ROWS == 0 for w in weights)
    return ([pltpu.VMEM(w.shape, BF16) for w in weights]
            + [pltpu.VMEM((WEIGHT_STAGES, WEIGHT_CHUNK_ROWS, widest), F32),
               pltpu.SemaphoreType.DMA((WEIGHT_STAGES,))])


def _resident(shape):
    return pl.BlockSpec(shape, lambda *_: (0,) * len(shape), pipeline_mode=pl.Buffered(1))


def _rotary(t, cos, sin_signed, even_lane):
    d = t.shape[-1]
    swapped = jnp.where(even_lane, pltpu.roll(t, d - 1, 1), pltpu.roll(t, 1, 1))
    return t * cos + swapped * sin_signed


def _ffn_in_kernel(n_steps, x_ref, g1_ref, wg_hbm, wu_hbm, wd_hbm, gm_ref, win_hbm, rot_ref, zeta_ref,
                   h_ref, r_ref, rg_ref, a1_ref, a4_ref, a16_ref,
                   xn_ref, act_ref, slab_ref, by4_ref, hn_ref, wg_ref, wu_ref, wd_ref, win_ref, stage_ref, sem):
    @pl.when(pl.program_id(0) == 0)
    def _():
        _load_weights_bf16([(wg_hbm, wg_ref), (wu_hbm, wu_ref), (wd_hbm, wd_ref), (win_hbm, win_ref)],
                           stage_ref, sem)
        hn_ref[1] = jnp.zeros(hn_ref.shape[1:], hn_ref.dtype)

    rq_ref, rk_ref, rkz_ref, rv_ref = (r_ref.at[:, j * RET_WIDTH:(j + 1) * RET_WIDTH] for j in range(4))

    def body(slot, lead):
        prev = hn_ref.at[1 - slot]
        _project_attention(prev, win_ref, a1_ref, slab_ref)
        _regroup_attention(slab_ref, by4_ref, a4_ref, a16_ref)
        if lead:
            x = x_ref[...]
            xn_ref[...] = _rms_norm(x, g1_ref[...]).astype(BF16)
            h = x + 0.5 * _swiglu_half_step(xn_ref, wg_ref, wu_ref, wd_ref, act_ref)
            h_ref[...] = h
            hn_ref[slot] = _rms_norm(h, gm_ref[...]).astype(BF16)
        _project_retention_qk(prev, win_ref, rot_ref, zeta_ref, rq_ref, rk_ref, rkz_ref)
        _project_retention_vg(prev, win_ref, rv_ref, rg_ref)

    _run_lagged(body, n_steps)


def _project_retention_qk(xn_ref, win_ref, rot_ref, zeta_ref, rq_ref, rk_ref, rkz_ref):
    rows = xn_ref.shape[0]
    hd = RET_HEAD_DIM
    even_lane = (lax.broadcasted_iota(jnp.int32, (rows, hd), 1) % 2) == 0
    cos = rot_ref[:, 0:hd]
    sin = rot_ref[:, hd:2 * hd]
    for c in range(RET_WIDTH // MXU_DIM):
        cols = slice(c * MXU_DIM, (c + 1) * MXU_DIM)
        uq = _dot(xn_ref[...], win_ref[:, cols])
        uk = _dot(xn_ref[...], win_ref[:, RET_WIDTH + c * MXU_DIM:RET_WIDTH + (c + 1) * MXU_DIM])
        for half in range(MXU_DIM // hd):
            lanes = slice(c * MXU_DIM + half * hd, c * MXU_DIM + (half + 1) * hd)
            part = slice(half * hd, (half + 1) * hd)
            rq_ref[:, lanes] = _rotary(uq[:, part], cos, sin, even_lane).astype(BF16)
            k = _rotary(uk[:, part], cos, sin, even_lane) * (RET_HEAD_DIM ** -0.5)
            rk_ref[:, lanes] = k.astype(BF16)
            rkz_ref[:, lanes] = (k * zeta_ref[:, lanes]).astype(BF16)


def _project_attention(xn_ref, win_ref, a1_ref, slab_ref):
    n_ret = 4 * RET_WIDTH
    for c in range(slab_ref.shape[0]):
        cols = slice(c * MXU_DIM, (c + 1) * MXU_DIM)
        u = _dot(xn_ref[...], win_ref[:, n_ret + c * MXU_DIM:n_ret + (c + 1) * MXU_DIM])
        if (c + 1) * MXU_DIM <= ATT_WIDTH:
            u = u * (ATT_HEAD_DIM ** -0.5)
        ub = u.astype(BF16)
        a1_ref[:, cols] = ub
        rounded = ub.astype(F32)
        slab_ref[c] = pltpu.pack_elementwise([rounded[:, 0:LANES], rounded[:, LANES:2 * LANES]],
                                             packed_dtype=BF16)


def _unpack_bf16_pair(words):
    return [pltpu.unpack_elementwise(words, index=i, packed_dtype=BF16, unpacked_dtype=F32).astype(BF16)
            for i in range(2)]


def _regroup_attention(slab_ref, by4_ref, a4_ref, a16_ref):
    rows = slab_ref.shape[1]
    d4, d16 = DILATIONS[1], DILATIONS[2]
    for s in range(slab_ref.shape[0]):
        lanes = [slice(s * MXU_DIM + h * LANES, s * MXU_DIM + (h + 1) * LANES) for h in range(2)]
        slab = slab_ref.at[s]
        by4 = by4_ref.at[s]
        for a in range(d4):
            words = slab[pl.ds(a, rows // d4, stride=d4), :]
            by4[a] = words
            for lane, part in zip(lanes, _unpack_bf16_pair(words)):
                a4_ref[0, a, :, lane] = part
        for res in range(d16):
            words = by4[res % d4, pl.ds(res // d4, rows // d16, stride=d16 // d4), :]
            for lane, part in zip(lanes, _unpack_bf16_pair(words)):
                a16_ref[0, res, :, lane] = part


def _project_retention_vg(xn_ref, win_ref, rv_ref, rg_ref):
    rv_ref[...] = _dot(xn_ref[...], win_ref[:, 2 * RET_WIDTH:3 * RET_WIDTH]).astype(BF16)
    rg_ref[...] = _dot(xn_ref[...], win_ref[:, 3 * RET_WIDTH:4 * RET_WIDTH])


def _ffn_in(x2, g1, wg, wu, wd, gm, win, rotation, zeta_rows, batch):
    t, d = x2.shape
    d_ff = wg.shape[1]
    n_ret = 4 * RET_WIDTH
    n_att = 3 * ATT_WIDTH
    rows = FFN_ROWS
    seq = t // batch
    per_seq = seq // rows
    d4, d16 = DILATIONS[1], DILATIONS[2]
    n_tiles = t // rows
    ffn_tile = lambda i: jnp.minimum(i, n_tiles - 1)
    proj_tile = lambda i: jnp.maximum(i - 1, 0)
    regroup = lambda dil: pl.BlockSpec((1, dil, rows // dil, n_att),
                                       lambda i: (proj_tile(i) // per_seq, 0, proj_tile(i) % per_seq, 0))
    by_rows = lambda width: pl.BlockSpec((rows, width), lambda i: (proj_tile(i), 0))
    position = pl.BlockSpec((rows, 2 * RET_HEAD_DIM), lambda i: (proj_tile(i) % per_seq, 0))
    in_hbm = pl.BlockSpec(memory_space=pl.ANY)
    return pl.pallas_call(
        functools.partial(_ffn_in_kernel, n_tiles + 1),
        grid=(n_tiles + 1,),
        in_specs=[
            pl.BlockSpec((rows, d), lambda i: (ffn_tile(i), 0)),
            _resident((1, d)),
            in_hbm, in_hbm, in_hbm,
            _resident((1, d)),
            in_hbm,
            position,
            _resident(zeta_rows.shape),
        ],
        out_specs=([pl.BlockSpec((rows, d), lambda i: (ffn_tile(i), 0)), by_rows(4 * RET_WIDTH),
                    by_rows(RET_WIDTH), by_rows(n_att), regroup(d4), regroup(d16)]),
        out_shape=(
            [jax.ShapeDtypeStruct((t, d), F32),
             jax.ShapeDtypeStruct((t, 4 * RET_WIDTH), BF16)]
            + [jax.ShapeDtypeStruct((t, RET_WIDTH), F32),
               jax.ShapeDtypeStruct((t, n_att), BF16),
               jax.ShapeDtypeStruct((batch, d4, seq // d4, n_att), BF16),
               jax.ShapeDtypeStruct((batch, d16, seq // d16, n_att), BF16)]),
        scratch_shapes=[pltpu.VMEM((rows, d), BF16), pltpu.VMEM((rows, d_ff), BF16),
                        pltpu.VMEM((n_att // MXU_DIM, rows, LANES), jnp.uint32),
                        pltpu.VMEM((n_att // MXU_DIM, d4, rows // d4, LANES), jnp.uint32),
                        pltpu.VMEM((2, rows, d), BF16)] + _weight_scratch(wg, wu, wd, win),
        compiler_params=pltpu.CompilerParams(
            dimension_semantics=("arbitrary",), vmem_limit_bytes=VMEM_LIMIT_BYTES),
        name="ffn_in",
    )(x2, g1, wg, wu, wd, gm, win, rotation, zeta_rows)


def _retention_tables(seq, tile_rows):
    d = RET_HEAD_DIM
    pos = jnp.arange(seq, dtype=F32)
    inv_freq = ROPE_BASE ** (-jnp.arange(0, d, 2, dtype=F32) / d)
    ang = pos[:, None] * jnp.repeat(inv_freq, 2)[None, :]
    even = (jnp.arange(d) % 2) == 0
    sin = jnp.sin(ang)
    rotation = jnp.concatenate([jnp.cos(ang), jnp.where(even[None, :], -sin, sin)], axis=1)

    c = RET_CHUNK
    log_g = jnp.log(1.0 - 2.0 ** (-5.0 - jnp.arange(RET_HEADS, dtype=F32)))
    idx = jnp.arange(c, dtype=F32)
    rel = idx[:, None] - idx[None, :]
    decay_in = jnp.where(rel >= 0, jnp.exp(log_g[:, None, None] * jnp.maximum(rel, 0.0)), 0.0)
    zeta = jnp.exp(log_g[:, None] * (c - 1 - idx)[None, :])
    xi = jnp.exp(log_g[:, None] * (idx + 1)[None, :])
    chunk_decay = jnp.exp(log_g * c)
    xi_lanes = jnp.broadcast_to(xi[:, :, None], (RET_HEADS, c, d))
    gamma = jnp.broadcast_to(chunk_decay[:, None, None], (RET_HEADS, 8, d))
    zeta_rows = jnp.tile(jnp.repeat(zeta.T, d, axis=1), (tile_rows // c, 1))
    return rotation, zeta_rows, decay_in, xi_lanes, gamma


def _retention_kernel(q_ref, k_ref, kz_ref, v_ref, dec_ref, xi_ref, gamma_ref, o_ref, state_ref):
    c = RET_CHUNK
    d = RET_HEAD_DIM

    @pl.when(pl.program_id(1) == 0)
    def _():
        state_ref[...] = jnp.zeros_like(state_ref)

    def chunk(j, carry):
        rows = pl.ds(pl.multiple_of(j * c, c), c)
        for h in range(RET_HEADS):
            lanes = slice(h * d, (h + 1) * d)
            qb = q_ref[0, rows, lanes]
            vb = v_ref[0, rows, lanes]
            scores = _dot_nt(qb, k_ref[0, rows, lanes]) * dec_ref[h]
            state = state_ref[h]
            o_ref[0, rows, lanes] = _dot(scores.astype(BF16), vb) + _dot(qb, state.astype(BF16)) * xi_ref[h]
            state_ref[h] = state * gamma_ref[h, 0:1, :] + _dot_tn(kz_ref[0, rows, lanes], vb)
        return carry

    lax.fori_loop(0, q_ref.shape[1] // c, chunk, 0, unroll=True)


def _retention(r, decay_in, xi_lanes, gamma):
    b, s, _ = r.shape
    w = RET_WIDTH
    rows = RET_ROWS
    d = RET_HEAD_DIM
    tile = lambda j: pl.BlockSpec((1, rows, w), lambda bi, si: (bi, si, j))
    return pl.pallas_call(
        _retention_kernel,
        grid=(b, s // rows),
        in_specs=[tile(0), tile(1), tile(2), tile(3),
                  _resident(decay_in.shape), _resident(xi_lanes.shape), _resident(gamma.shape)],
        out_specs=tile(0),
        out_shape=jax.ShapeDtypeStruct((b, s, w), F32),
        scratch_shapes=[pltpu.VMEM((RET_HEADS, d, d), F32)],
        compiler_params=pltpu.CompilerParams(
            dimension_semantics=("arbitrary", "arbitrary"), vmem_limit_bytes=VMEM_LIMIT_BYTES),
        name="retention",
    )(r, r, r, r, decay_in, xi_lanes, gamma)


ATT_BLOCKS = ATT_ROWS // BAND_BLOCK
KIND_OUT, KIND_LSE = 0, 1
N_KINDS = 2


def _attention_bias():
    blk = BAND_BLOCK
    qi = np.arange(blk)[:, None]
    kj = np.arange(2 * blk)[None, :]
    dist = qi + blk - kj
    band = (dist >= 0) & (dist <= ATT_SPAN)
    masks = np.stack([band, band & (kj >= blk)])
    return jnp.asarray(np.where(masks, 0.0, -np.inf).astype(np.float32))


def _dilated_kernel(tiles_per_seq, n_steps, *refs):
    res_ref, y_ref = refs[-2:]
    vbufs = refs[-7:-4]
    step = pl.program_id(0)
    tile = step % tiles_per_seq

    @pl.when(step == 0)
    def _():
        for vbuf in vbufs:
            vbuf[:, LANES:2 * LANES] = jnp.ones((vbuf.shape[0], LANES), BF16)
        res_ref[1] = jnp.ones(res_ref.shape[1:], res_ref.dtype)
        y_ref[1] = jnp.ones(y_ref.shape[1:], y_ref.dtype)

    _run_lagged(lambda slot, lead: _dilated_body(slot, lead, tile, *refs), n_steps)


def _dilated_body(slot, lead, tile, q1, k1c, k1p, v1c, v1p, q4, k4c, k4p, v4c, v4p, q16, k16c, k16p, v16c, v16p,
                  bias_ref, o_ref, kb0, kb1, kb2, vb0, vb1, vb2, s_ref, m_ref, res_ref, y_ref):
    rows = ATT_ROWS
    blk = BAND_BLOCK
    hd = ATT_HEAD_DIM
    n_br = len(DILATIONS)
    kbufs, vbufs = (kb0, kb1, kb2), (vb0, vb1, vb2)
    d4, d16 = DILATIONS[1], DILATIONS[2]
    res_new, y_new = res_ref.at[slot], y_ref.at[slot]
    res_old, y_old = res_ref.at[1 - slot], y_ref.at[1 - slot]

    per_branch = (
        ([k1p.at[0]], [k1c.at[0]], [v1p.at[0]], [v1c.at[0]]),
        ([k4p.at[0, r] for r in range(d4)], [k4c.at[0, r] for r in range(d4)],
         [v4p.at[0, r] for r in range(d4)], [v4c.at[0, r] for r in range(d4)]),
        ([k16p.at[0, r] for r in range(d16)], [k16c.at[0, r] for r in range(d16)],
         [v16p.at[0, r] for r in range(d16)], [v16c.at[0, r] for r in range(d16)]),
    )
    for br, (kps, kcs, vps, vcs) in enumerate(per_branch if lead else ()):
        cur = rows // len(kps)
        for r in range(len(kps)):
            base = r * (blk + cur)
            kbufs[br][base:base + blk, :] = kps[r][...]
            kbufs[br][base + blk:base + blk + cur, :] = kcs[r][...]
            vbufs[br][base:base + blk, 0:LANES] = vps[r][...]
            vbufs[br][base + blk:base + blk + cur, 0:LANES] = vcs[r][...]

    head0 = lax.broadcasted_iota(jnp.int32, (blk, LANES), 1) < hd

    per_res = (ATT_BLOCKS, ATT_BLOCKS // d4, ATT_BLOCKS // d16)
    residue = lambda br, idx: idx // per_res[br]
    block_in_residue = lambda br, idx: idx % per_res[br]
    key_rows = lambda br, idx: slice((idx + residue(br, idx)) * blk, (idx + residue(br, idx) + 2) * blk)

    def load_q(br, idx):
        res, jb = residue(br, idx), block_in_residue(br, idx)
        if br == 0:
            return q1[0, jb * blk:(jb + 1) * blk, :]
        return (q4, q16)[br - 1][0, res, jb * blk:(jb + 1) * blk, :]

    def store_result(br, idx, kind, val):
        res, jb = residue(br, idx), block_in_residue(br, idx)
        if br == 0:
            res_new[kind, idx * blk:(idx + 1) * blk, :] = val
        elif br == 1:
            res_new[N_KINDS + kind, pl.ds(res + d4 * blk * jb, blk, stride=d4), :] = val
        else:
            y_new[kind, pl.ds((res % d4) * (rows // d4) + res // d4, blk, stride=d4), :] = val

    def pair(lo, hi):
        return jnp.where(head0, lo, hi)

    def scores(idx):
        slot = idx % 2
        for br in range(n_br):
            q = load_q(br, idx)
            zero = jnp.zeros_like(q)
            q2 = jnp.concatenate([jnp.where(head0, q, zero), jnp.where(head0, zero, q)], axis=0)
            s = _dot_nt(q2, kbufs[br][key_rows(br, idx), :])
            first = (tile == 0).astype(jnp.int32) if block_in_residue(br, idx) == 0 else 0
            bias = bias_ref[first]
            for h in range(2):
                r = slice(h * blk, (h + 1) * blk)
                sh = s[r] + bias
                s_ref[slot, br, r, :] = sh
                m_ref[slot, br, h] = jnp.broadcast_to(jnp.max(sh, axis=-1, keepdims=True), (blk, LANES))

    def values(idx):
        slot = idx % 2
        for br in range(n_br):
            parts, maxima = [], []
            for h in range(2):
                m = m_ref[slot, br, h]
                s = s_ref[slot, br, h * blk:(h + 1) * blk, :]
                parts.append(jnp.exp(s - jnp.concatenate([m, m], axis=1)).astype(BF16))
                maxima.append(m)
            pv = _dot(jnp.concatenate(parts, axis=0), vbufs[br][key_rows(br, idx), :])
            acc = pair(pv[0:blk, 0:LANES], pv[blk:2 * blk, 0:LANES])
            den = pair(pv[0:blk, LANES:2 * LANES], pv[blk:2 * blk, LANES:2 * LANES])
            store_result(br, idx, KIND_OUT, acc / den)
            store_result(br, idx, KIND_LSE, pair(*maxima) + jnp.log(den))

    sub = rows // d4

    def regroup_old(c):
        for kind in range(N_KINDS):
            for a in range(d4):
                res_old[2 * N_KINDS + kind, pl.ds(a + d4 * blk * c, blk, stride=d4), :] = (
                    y_old[kind, a * sub + c * blk:a * sub + (c + 1) * blk, :])

    def combine_old(i):
        r = slice(i * blk, (i + 1) * blk)
        lses = [res_old[br * N_KINDS + KIND_LSE, r, :] for br in range(n_br)]
        mx = jnp.maximum(jnp.maximum(lses[0], lses[1]), lses[2])
        ws = [jnp.exp(lse - mx) for lse in lses]
        num = sum(w * res_old[br * N_KINDS + KIND_OUT, r, :] for br, w in enumerate(ws))
        o_ref[0, r, :] = (num / (ws[0] + ws[1] + ws[2])).astype(o_ref.dtype)

    if lead:
        scores(0)
        scores(1)
        for i in range(ATT_BLOCKS):
            values(i)
            if i + 2 < ATT_BLOCKS:
                scores(i + 2)
    for i in range(ATT_BLOCKS):
        if i % d4 == 0:
            regroup_old(i // d4)
        combine_old(i)


def _dilated(a1, a4, a16, bias):
    b, s, _ = a1.shape
    rows = ATT_ROWS
    blk = BAND_BLOCK
    pairs = ATT_WIDTH // LANES
    d4, d16 = DILATIONS[1], DILATIONS[2]
    tiles = s // rows
    n_steps = b * pairs * tiles
    decode = lambda i: (i // (pairs * tiles), (i // tiles) % pairs, i % tiles)
    of_compute = lambda f: (lambda i: f(*decode(jnp.minimum(i, n_steps - 1))))
    of_combine = lambda f: (lambda i: f(*decode(jnp.maximum(i - 1, 0))))
    prev_of = lambda n: (lambda i: jnp.maximum(i * n - 1, 0))

    def specs(dil):
        sub = rows // dil
        per = sub // blk
        if dil == 1:
            cur = lambda off: pl.BlockSpec((1, sub, LANES), of_compute(lambda bi, pi, ti: (bi, ti, off + pi)))
            prv = lambda off: pl.BlockSpec(
                (1, blk, LANES), of_compute(lambda bi, pi, ti: (bi, prev_of(per)(ti), off + pi)))
        else:
            cur = lambda off: pl.BlockSpec(
                (1, dil, sub, LANES), of_compute(lambda bi, pi, ti: (bi, 0, ti, off + pi)))
            prv = lambda off: pl.BlockSpec(
                (1, dil, blk, LANES), of_compute(lambda bi, pi, ti: (bi, 0, prev_of(per)(ti), off + pi)))
        return [cur(0), cur(pairs), prv(pairs), cur(2 * pairs), prv(2 * pairs)]

    key_rows = lambda dil: (ATT_BLOCKS + dil) * blk
    return pl.pallas_call(
        functools.partial(_dilated_kernel, tiles, n_steps + 1),
        grid=(n_steps + 1,),
        in_specs=specs(1) + specs(d4) + specs(d16) + [_resident(bias.shape)],
        out_specs=pl.BlockSpec((1, rows, LANES), of_combine(lambda bi, pi, ti: (bi, ti, pi))),
        out_shape=jax.ShapeDtypeStruct((b, s, ATT_WIDTH), BF16),
        scratch_shapes=(
            [pltpu.VMEM((key_rows(dil), LANES), BF16) for dil in DILATIONS]
            + [pltpu.VMEM((key_rows(dil), 2 * LANES), BF16) for dil in DILATIONS]
            + [pltpu.VMEM((2, len(DILATIONS), 2 * blk, 2 * blk), F32),
               pltpu.VMEM((2, len(DILATIONS), 2, blk, LANES), F32),
               pltpu.VMEM((2, len(DILATIONS) * N_KINDS, rows, LANES), F32),
               pltpu.VMEM((2, N_KINDS, rows, LANES), F32)]),
        compiler_params=pltpu.CompilerParams(
            dimension_semantics=("arbitrary",), vmem_limit_bytes=VMEM_LIMIT_BYTES),
        name="dilated",
    )(a1, a1, a1, a1, a1, a4, a4, a4, a4, a4, a16, a16, a16, a16, a16, bias)


def _out_ffn_kernel(h_ref, ret_ref, gate_ref, att_ref, gain_ref, wo_hbm, g2_ref, wg_hbm, wu_hbm, wd_hbm, gf_ref,
                    y_ref, xn_ref, act_ref, mix_ref, h2_ref, wo_ref, wg_ref, wu_ref, wd_ref, stage_ref, sem,
                    *, final_norm, n_steps):
    d = RET_HEAD_DIM

    @pl.when(pl.program_id(0) == 0)
    def _():
        _load_weights_bf16([(wo_hbm, wo_ref), (wg_hbm, wg_ref), (wu_hbm, wu_ref), (wd_hbm, wd_ref)],
                           stage_ref, sem)
        h2_ref[1] = jnp.zeros(h2_ref.shape[1:], h2_ref.dtype)

    def body(slot, lead):
        if lead:
            h2_ref[slot] = h_ref[...] + _dot(att_ref[...], wo_ref[RET_WIDTH:, :])
        h_prev = h2_ref[1 - slot]
        xn_ref[...] = _rms_norm(h_prev, g2_ref[...]).astype(BF16)
        if lead:
            for hd in range(RET_HEADS):
                lanes = slice(hd * d, (hd + 1) * d)
                ret = ret_ref[:, lanes]
                cen = ret - jnp.mean(ret, axis=-1, keepdims=True)
                var = jnp.mean(cen * cen, axis=-1, keepdims=True)
                y = cen * lax.rsqrt(var + GN_EPS) * gain_ref[:, lanes]
                gate = gate_ref[:, lanes]
                mix_ref[:, lanes] = (y * (gate * jax.nn.sigmoid(gate))).astype(BF16)
        out = h2_ref[1 - slot] + 0.5 * _swiglu_half_step(xn_ref, wg_ref, wu_ref, wd_ref, act_ref)
        if lead:
            h2_ref[slot] = h2_ref[slot] + _dot(mix_ref[...], wo_ref[0:RET_WIDTH, :])
        if final_norm:
            out = _rms_norm(out, gf_ref[...])
        y_ref[...] = out

    _run_lagged(body, n_steps)


def _out_ffn(h1, ret, gate, att, gain, wo, g2, wg, wu, wd, gf, final_norm):
    t, d = h1.shape
    d_ff = wg.shape[1]
    rows = FFN_ROWS
    n_tiles = t // rows
    by_rows = lambda a: pl.BlockSpec((rows, a.shape[1]), lambda i: (jnp.minimum(i, n_tiles - 1), 0))
    in_hbm = pl.BlockSpec(memory_space=pl.ANY)
    return pl.pallas_call(
        functools.partial(_out_ffn_kernel, final_norm=final_norm, n_steps=n_tiles + 1),
        grid=(n_tiles + 1,),
        in_specs=[
            by_rows(h1), by_rows(ret), by_rows(gate), by_rows(att),
            _resident(gain.shape),
            in_hbm,
            _resident((1, d)),
            in_hbm, in_hbm, in_hbm,
            _resident((1, d)),
        ],
        out_specs=pl.BlockSpec((rows, d), lambda i: (jnp.maximum(i - 1, 0), 0)),
        out_shape=jax.ShapeDtypeStruct((t, d), F32),
        scratch_shapes=[pltpu.VMEM((rows, d), BF16), pltpu.VMEM((rows, d_ff), BF16),
                        pltpu.VMEM((rows, RET_WIDTH), BF16), pltpu.VMEM((2, rows, d), F32)]
        + _weight_scratch(wo, wg, wu, wd),
        compiler_params=pltpu.CompilerParams(
            dimension_semantics=("arbitrary",), vmem_limit_bytes=VMEM_LIMIT_BYTES),
        name="out_ffn",
    )(h1, ret, gate, att, gain, wo, g2, wg, wu, wd, gf)


def kernel(x, norm_ffn1, ffn1_w_gate, ffn1_w_up, ffn1_w_down, norm_mix, w_in, ret_norm_gain,
           w_out, norm_ffn2, ffn2_w_gate, ffn2_w_up, ffn2_w_down, norm_final):
    b, s, d = x.shape
    depth = norm_ffn1.shape[0]
    assert s % ATT_ROWS == 0 and s % RET_ROWS == 0 and (b * s) % FFN_ROWS == 0
    assert w_in.shape[2] == 4 * RET_WIDTH + 3 * ATT_WIDTH
    assert FFN_ROWS % RET_CHUNK == 0 and RET_ROWS % RET_CHUNK == 0
    rotation, zeta_rows, decay_in, xi_lanes, gamma = _retention_tables(s, FFN_ROWS)
    bias = _attention_bias()
    row = lambda v: v.reshape(1, -1)
    seq = lambda a: a.reshape(b, s, -1)

    h = x.reshape(b * s, d)
    for l in range(depth):
        h1, r, rg, a1, a4, a16 = _ffn_in(
            h, row(norm_ffn1[l]), ffn1_w_gate[l], ffn1_w_up[l], ffn1_w_down[l],
            row(norm_mix[l]), w_in[l], rotation, zeta_rows, b)
        ret = _retention(seq(r), decay_in, xi_lanes, gamma)
        att = _dilated(seq(a1), a4, a16, bias)
        h = _out_ffn(h1, ret.reshape(b * s, -1), rg, att.reshape(b * s, -1), row(ret_norm_gain[l]),
                     w_out[l], row(norm_ffn2[l]), ffn2_w_gate[l], ffn2_w_up[l], ffn2_w_down[l],
                     row(norm_final), final_norm=(l == depth - 1))
    return h.reshape(b, s, d)
```

```python
import functools

import jax
import jax.numpy as jnp
import numpy as np
from jax import lax
from jax.experimental import pallas as pl
from jax.experimental.pallas import tpu as pltpu

RET_HEADS = 4
RET_HEAD_DIM = 128
RET_WIDTH = RET_HEADS * RET_HEAD_DIM
ATT_HEADS = 8
ATT_HEAD_DIM = 64
ATT_WIDTH = ATT_HEADS * ATT_HEAD_DIM
DILATIONS = (1, 4, 16)
ATT_SPAN = 128
BAND_BLOCK = 128
ROPE_BASE = 10000.0
NORM_EPS = 1e-6
GN_EPS = 1e-6

LANES = 128
MXU_DIM = 256
VMEM_LIMIT_BYTES = 56 * 1024 * 1024
WEIGHT_CHUNK_ROWS = 128
WEIGHT_STAGES = 4

FFN_ROWS = 256
OUT_FFN_ROWS = 512
FF_CHUNK = MXU_DIM
RET_ROWS = 4096
RET_CHUNK = MXU_DIM
ATT_ROWS = ATT_SPAN * DILATIONS[-1]

F32 = jnp.float32
BF16 = jnp.bfloat16


def _dot(a, b):
    return jnp.dot(a, b, preferred_element_type=F32)


def _dot_nt(a, b):
    return lax.dot_general(a, b, (((1,), (1,)), ((), ())), preferred_element_type=F32)


def _dot_tn(a, b):
    return lax.dot_general(a, b, (((0,), (0,)), ((), ())), preferred_element_type=F32)


def _rms_norm(x, gain):
    return x * lax.rsqrt(jnp.mean(x * x, axis=-1, keepdims=True) + NORM_EPS) * gain


def _swiglu_half_step(xn_ref, wg_ref, wu_ref, wd_ref, act_ref, rows=slice(None)):
    d_ff = wg_ref.shape[1]
    for c in range(d_ff // FF_CHUNK):
        cols = slice(c * FF_CHUNK, (c + 1) * FF_CHUNK)
        g = _dot(xn_ref[rows, :], wg_ref[:, cols])
        u = _dot(xn_ref[rows, :], wu_ref[:, cols])
        act_ref[rows, cols] = (g * jax.nn.sigmoid(g) * u).astype(BF16)
    return _dot(act_ref[rows, :], wd_ref[...])


def _run_lagged(body, n_steps):
    step = pl.program_id(0)
    last = n_steps - 1
    for slot in range(2):
        pl.when((jnp.bitwise_and(step, 1) == slot) & (step < last))(functools.partial(body, slot, True))
    pl.when(step == last)(functools.partial(body, last % 2, False))


def _load_weights_bf16(pairs, stage_ref, sem):
    rows = WEIGHT_CHUNK_ROWS
    depth = stage_ref.shape[0]
    chunks = [(src, dst, c) for src, dst in pairs for c in range(src.shape[0] // rows)]

    def copy(k):
        src, _, c = chunks[k]
        return pltpu.make_async_copy(src.at[c * rows:(c + 1) * rows, :],
                                     stage_ref.at[k % depth, :, 0:src.shape[1]], sem.at[k % depth])

    for k in range(min(depth - 1, len(chunks))):
        copy(k).start()
    for k, (src, dst, c) in enumerate(chunks):
        if k + depth - 1 < len(chunks):
            copy(k + depth - 1).start()
        copy(k).wait()
        dst[c * rows:(c + 1) * rows, :] = stage_ref[k % depth, :, 0:src.shape[1]].astype(BF16)


def _weight_scratch(*weights):
    widest = max(w.shape[1] for w in weights)
    assert all(w.dtype == F32 and w.shape[0] % WEIGHT_CHUNK_ROWS == 0 for w in weights)
    return ([pltpu.VMEM(w.shape, BF16) for w in weights]
            + [pltpu.VMEM((WEIGHT_STAGES, WEIGHT_CHUNK_ROWS, widest), F32),
               pltpu.SemaphoreType.DMA((WEIGHT_STAGES,))])


def _resident(shape):
    return pl.BlockSpec(shape, lambda *_: (0,) * len(shape), pipeline_mode=pl.Buffered(1))


def _rotary(t, cos, sin_signed, even_lane):
    d = t.shape[-1]
    swapped = jnp.where(even_lane, pltpu.roll(t, d - 1, 1), pltpu.roll(t, 1, 1))
    return t * cos + swapped * sin_signed


def _ffn_in_kernel(n_steps, x_ref, g1_ref, wg_hbm, wu_hbm, wd_hbm, gm_ref, win_hbm, rot_ref, zeta_ref,
                   h_ref, r_ref, rg_ref, a1_ref, a4_ref, a16_ref,
                   xn_ref, act_ref, slab_ref, by4_ref, hn_ref, wg_ref, wu_ref, wd_ref, win_ref, stage_ref, sem):
    @pl.when(pl.program_id(0) == 0)
    def _():
        _load_weights_bf16([(wg_hbm, wg_ref), (wu_hbm, wu_ref), (wd_hbm, wd_ref), (win_hbm, win_ref)],
                           stage_ref, sem)
        hn_ref[1] = jnp.zeros(hn_ref.shape[1:], hn_ref.dtype)

    rq_ref, rk_ref, rkz_ref, rv_ref = (r_ref.at[:, j * RET_WIDTH:(j + 1) * RET_WIDTH] for j in range(4))

    def body(slot, lead):
        prev = hn_ref.at[1 - slot]
        _project_attention(prev, win_ref, a1_ref, slab_ref)
        _regroup_attention(slab_ref, by4_ref, a4_ref, a16_ref)
        if lead:
            x = x_ref[...]
            xn_ref[...] = _rms_norm(x, g1_ref[...]).astype(BF16)
            h = x + 0.5 * _swiglu_half_step(xn_ref, wg_ref, wu_ref, wd_ref, act_ref)
            h_ref[...] = h
            hn_ref[slot] = _rms_norm(h, gm_ref[...]).astype(BF16)
        _project_retention_qk(prev, win_ref, rot_ref, zeta_ref, rq_ref, rk_ref, rkz_ref)
        _project_retention_vg(prev, win_ref, rv_ref, rg_ref)

    _run_lagged(body, n_steps)


def _project_retention_qk(xn_ref, win_ref, rot_ref, zeta_ref, rq_ref, rk_ref, rkz_ref):
    rows = xn_ref.shape[0]
    hd = RET_HEAD_DIM
    even_lane = (lax.broadcasted_iota(jnp.int32, (rows, hd), 1) % 2) == 0
    cos = rot_ref[:, 0:hd]
    sin = rot_ref[:, hd:2 * hd]
    for c in range(RET_WIDTH // MXU_DIM):
        cols = slice(c * MXU_DIM, (c + 1) * MXU_DIM)
        uq = _dot(xn_ref[...], win_ref[:, cols])
        uk = _dot(xn_ref[...], win_ref[:, RET_WIDTH + c * MXU_DIM:RET_WIDTH + (c + 1) * MXU_DIM])
        for half in range(MXU_DIM // hd):
            lanes = slice(c * MXU_DIM + half * hd, c * MXU_DIM + (half + 1) * hd)
            part = slice(half * hd, (half + 1) * hd)
            rq_ref[:, lanes] = _rotary(uq[:, part], cos, sin, even_lane).astype(BF16)
            k = _rotary(uk[:, part], cos, sin, even_lane) * (RET_HEAD_DIM ** -0.5)
            rk_ref[:, lanes] = k.astype(BF16)
            rkz_ref[:, lanes] = (k * zeta_ref[:, lanes]).astype(BF16)


def _project_attention(xn_ref, win_ref, a1_ref, slab_ref):
    n_ret = 4 * RET_WIDTH
    for c in range(slab_ref.shape[0]):
        cols = slice(c * MXU_DIM, (c + 1) * MXU_DIM)
        u = _dot(xn_ref[...], win_ref[:, n_ret + c * MXU_DIM:n_ret + (c + 1) * MXU_DIM])
        if (c + 1) * MXU_DIM <= ATT_WIDTH:
            u = u * (ATT_HEAD_DIM ** -0.5)
        ub = u.astype(BF16)
        a1_ref[:, cols] = ub
        rounded = ub.astype(F32)
        slab_ref[c] = pltpu.pack_elementwise([rounded[:, 0:LANES], rounded[:, LANES:2 * LANES]],
                                             packed_dtype=BF16)


def _unpack_bf16_pair(words):
    return [pltpu.unpack_elementwise(words, index=i, packed_dtype=BF16, unpacked_dtype=F32).astype(BF16)
            for i in range(2)]


def _regroup_attention(slab_ref, by4_ref, a4_ref, a16_ref):
    rows = slab_ref.shape[1]
    d4, d16 = DILATIONS[1], DILATIONS[2]
    for s in range(slab_ref.shape[0]):
        lanes = [slice(s * MXU_DIM + h * LANES, s * MXU_DIM + (h + 1) * LANES) for h in range(2)]
        slab = slab_ref.at[s]
        by4 = by4_ref.at[s]
        for a in range(d4):
            words = slab[pl.ds(a, rows // d4, stride=d4), :]
            by4[a] = words
            for lane, part in zip(lanes, _unpack_bf16_pair(words)):
                a4_ref[0, a, :, lane] = part
        for res in range(d16):
            words = by4[res % d4, pl.ds(res // d4, rows // d16, stride=d16 // d4), :]
            for lane, part in zip(lanes, _unpack_bf16_pair(words)):
                a16_ref[0, res, :, lane] = part


def _project_retention_vg(xn_ref, win_ref, rv_ref, rg_ref):
    rv_ref[...] = _dot(xn_ref[...], win_ref[:, 2 * RET_WIDTH:3 * RET_WIDTH]).astype(BF16)
    rg_ref[...] = _dot(xn_ref[...], win_ref[:, 3 * RET_WIDTH:4 * RET_WIDTH])


def _ffn_in(x2, g1, wg, wu, wd, gm, win, rotation, zeta_rows, batch):
    t, d = x2.shape
    d_ff = wg.shape[1]
    n_ret = 4 * RET_WIDTH
    n_att = 3 * ATT_WIDTH
    rows = FFN_ROWS
    seq = t // batch
    per_seq = seq // rows
    d4, d16 = DILATIONS[1], DILATIONS[2]
    n_tiles = t // rows
    ffn_tile = lambda i: jnp.minimum(i, n_tiles - 1)
    proj_tile = lambda i: jnp.maximum(i - 1, 0)
    regroup = lambda dil: pl.BlockSpec((1, dil, rows // dil, n_att),
                                       lambda i: (proj_tile(i) // per_seq, 0, proj_tile(i) % per_seq, 0))
    by_rows = lambda width: pl.BlockSpec((rows, width), lambda i: (proj_tile(i), 0))
    position = pl.BlockSpec((rows, 2 * RET_HEAD_DIM), lambda i: (proj_tile(i) % per_seq, 0))
    in_hbm = pl.BlockSpec(memory_space=pl.ANY)
    return pl.pallas_call(
        functools.partial(_ffn_in_kernel, n_tiles + 1),
        grid=(n_tiles + 1,),
        in_specs=[
            pl.BlockSpec((rows, d), lambda i: (ffn_tile(i), 0)),
            _resident((1, d)),
            in_hbm, in_hbm, in_hbm,
            _resident((1, d)),
            in_hbm,
            position,
            _resident(zeta_rows.shape),
        ],
        out_specs=([pl.BlockSpec((rows, d), lambda i: (ffn_tile(i), 0)), by_rows(4 * RET_WIDTH),
                    by_rows(RET_WIDTH), by_rows(n_att), regroup(d4), regroup(d16)]),
        out_shape=(
            [jax.ShapeDtypeStruct((t, d), F32),
             jax.ShapeDtypeStruct((t, 4 * RET_WIDTH), BF16)]
            + [jax.ShapeDtypeStruct((t, RET_WIDTH), F32),
               jax.ShapeDtypeStruct((t, n_att), BF16),
               jax.ShapeDtypeStruct((batch, d4, seq // d4, n_att), BF16),
               jax.ShapeDtypeStruct((batch, d16, seq // d16, n_att), BF16)]),
        scratch_shapes=[pltpu.VMEM((rows, d), BF16), pltpu.VMEM((rows, d_ff), BF16),
                        pltpu.VMEM((n_att // MXU_DIM, rows, LANES), jnp.uint32),
                        pltpu.VMEM((n_att // MXU_DIM, d4, rows // d4, LANES), jnp.uint32),
                        pltpu.VMEM((2, rows, d), BF16)] + _weight_scratch(wg, wu, wd, win),
        compiler_params=pltpu.CompilerParams(
            dimension_semantics=("arbitrary",), vmem_limit_bytes=VMEM_LIMIT_BYTES),
        name="ffn_in",
    )(x2, g1, wg, wu, wd, gm, win, rotation, zeta_rows)


def _retention_tables(seq, tile_rows):
    d = RET_HEAD_DIM
    pos = jnp.arange(seq, dtype=F32)
    inv_freq = ROPE_BASE ** (-jnp.arange(0, d, 2, dtype=F32) / d)
    ang = pos[:, None] * jnp.repeat(inv_freq, 2)[None, :]
    even = (jnp.arange(d) % 2) == 0
    sin = jnp.sin(ang)
    rotation = jnp.concatenate([jnp.cos(ang), jnp.where(even[None, :], -sin, sin)], axis=1)

    c = RET_CHUNK
    log_g = jnp.log(1.0 - 2.0 ** (-5.0 - jnp.arange(RET_HEADS, dtype=F32)))
    idx = jnp.arange(c, dtype=F32)
    rel = idx[:, None] - idx[None, :]
    decay_in = jnp.where(rel >= 0, jnp.exp(log_g[:, None, None] * jnp.maximum(rel, 0.0)), 0.0)
    zeta = jnp.exp(log_g[:, None] * (c - 1 - idx)[None, :])
    xi = jnp.exp(log_g[:, None] * (idx + 1)[None, :])
    chunk_decay = jnp.exp(log_g * c)
    xi_lanes = jnp.broadcast_to(xi[:, :, None], (RET_HEADS, c, d))
    gamma = jnp.broadcast_to(chunk_decay[:, None, None], (RET_HEADS, 8, d))
    zeta_rows = jnp.tile(jnp.repeat(zeta.T, d, axis=1), (tile_rows // c, 1))
    return rotation, zeta_rows, decay_in, xi_lanes, gamma


def _retention_kernel(q_ref, k_ref, kz_ref, v_ref, dec_ref, xi_ref, gamma_ref, o_ref, state_ref):
    c = RET_CHUNK
    d = RET_HEAD_DIM

    @pl.when(pl.program_id(1) == 0)
    def _():
        state_ref[...] = jnp.zeros_like(state_ref)

    def chunk(j, carry):
        rows = pl.ds(pl.multiple_of(j * c, c), c)
        for h in range(RET_HEADS):
            lanes = slice(h * d, (h + 1) * d)
            qb = q_ref[0, rows, lanes]
            vb = v_ref[0, rows, lanes]
            scores = _dot_nt(qb, k_ref[0, rows, lanes]) * dec_ref[h]
            state = state_ref[h]
            o_ref[0, rows, lanes] = _dot(scores.astype(BF16), vb) + _dot(qb, state.astype(BF16)) * xi_ref[h]
            state_ref[h] = state * gamma_ref[h, 0:1, :] + _dot_tn(kz_ref[0, rows, lanes], vb)
        return carry

    lax.fori_loop(0, q_ref.shape[1] // c, chunk, 0, unroll=True)


def _retention(r, decay_in, xi_lanes, gamma):
    b, s, _ = r.shape
    w = RET_WIDTH
    rows = RET_ROWS
    d = RET_HEAD_DIM
    tile = lambda j: pl.BlockSpec((1, rows, w), lambda bi, si: (bi, si, j))
    return pl.pallas_call(
        _retention_kernel,
        grid=(b, s // rows),
        in_specs=[tile(0), tile(1), tile(2), tile(3),
                  _resident(decay_in.shape), _resident(xi_lanes.shape), _resident(gamma.shape)],
        out_specs=tile(0),
        out_shape=jax.ShapeDtypeStruct((b, s, w), F32),
        scratch_shapes=[pltpu.VMEM((RET_HEADS, d, d), F32)],
        compiler_params=pltpu.CompilerParams(
            dimension_semantics=("arbitrary", "arbitrary"), vmem_limit_bytes=VMEM_LIMIT_BYTES),
        name="retention",
    )(r, r, r, r, decay_in, xi_lanes, gamma)


ATT_BLOCKS = ATT_ROWS // BAND_BLOCK
KIND_ACC, KIND_M, KIND_L = 0, 1, 2
N_KINDS = 3


def _attention_bias():
    blk = BAND_BLOCK
    qi = np.arange(blk)[:, None]
    kj = np.arange(2 * blk)[None, :]
    dist = qi + blk - kj
    band = (dist >= 0) & (dist <= ATT_SPAN)
    masks = np.stack([band, band & (kj >= blk)])
    return jnp.asarray(np.where(masks, 0.0, -np.inf).astype(np.float32))


def _dilated_kernel(tiles_per_seq, n_steps, *refs):
    res_ref, y_ref = refs[-2:]
    vbufs = refs[-7:-4]
    step = pl.program_id(0)
    tile = step % tiles_per_seq

    @pl.when(step == 0)
    def _():
        for vbuf in vbufs:
            vbuf[:, LANES:2 * LANES] = jnp.ones((vbuf.shape[0], LANES), BF16)
        res_ref[1] = jnp.ones(res_ref.shape[1:], res_ref.dtype)
        y_ref[1] = jnp.ones(y_ref.shape[1:], y_ref.dtype)

    _run_lagged(lambda slot, lead: _dilated_body(slot, lead, tile, *refs), n_steps)


def _dilated_body(slot, lead, tile, q1, k1c, k1p, v1c, v1p, q4, k4c, k4p, v4c, v4p, q16, k16c, k16p, v16c, v16p,
                  bias_ref, o_ref, kb0, kb1, kb2, vb0, vb1, vb2, s_ref, m_ref, res_ref, y_ref):
    rows = ATT_ROWS
    blk = BAND_BLOCK
    hd = ATT_HEAD_DIM
    n_br = len(DILATIONS)
    kbufs, vbufs = (kb0, kb1, kb2), (vb0, vb1, vb2)
    d4, d16 = DILATIONS[1], DILATIONS[2]
    res_new, y_new = res_ref.at[slot], y_ref.at[slot]
    res_old, y_old = res_ref.at[1 - slot], y_ref.at[1 - slot]

    per_branch = (
        ([k1p.at[0]], [k1c.at[0]], [v1p.at[0]], [v1c.at[0]]),
        ([k4p.at[0, r] for r in range(d4)], [k4c.at[0, r] for r in range(d4)],
         [v4p.at[0, r] for r in range(d4)], [v4c.at[0, r] for r in range(d4)]),
        ([k16p.at[0, r] for r in range(d16)], [k16c.at[0, r] for r in range(d16)],
         [v16p.at[0, r] for r in range(d16)], [v16c.at[0, r] for r in range(d16)]),
    )
    for br, (kps, kcs, vps, vcs) in enumerate(per_branch if lead else ()):
        cur = rows // len(kps)
        for r in range(len(kps)):
            base = r * (blk + cur)
            kbufs[br][base:base + blk, :] = kps[r][...]
            kbufs[br][base + blk:base + blk + cur, :] = kcs[r][...]
            vbufs[br][base:base + blk, 0:LANES] = vps[r][...]
            vbufs[br][base + blk:base + blk + cur, 0:LANES] = vcs[r][...]

    head0 = lax.broadcasted_iota(jnp.int32, (blk, LANES), 1) < hd

    per_res = (ATT_BLOCKS, ATT_BLOCKS // d4, ATT_BLOCKS // d16)
    residue = lambda br, idx: idx // per_res[br]
    block_in_residue = lambda br, idx: idx % per_res[br]
    key_rows = lambda br, idx: slice((idx + residue(br, idx)) * blk, (idx + residue(br, idx) + 2) * blk)

    def load_q(br, idx):
        res, jb = residue(br, idx), block_in_residue(br, idx)
        if br == 0:
            return q1[0, jb * blk:(jb + 1) * blk, :]
        return (q4, q16)[br - 1][0, res, jb * blk:(jb + 1) * blk, :]

    def store_result(br, idx, kind, val):
        res, jb = residue(br, idx), block_in_residue(br, idx)
        if br == 0:
            res_new[kind, idx * blk:(idx + 1) * blk, :] = val
        elif br == 1:
            res_new[N_KINDS + kind, pl.ds(res + d4 * blk * jb, blk, stride=d4), :] = val
        else:
            y_new[kind, pl.ds((res % d4) * (rows // d4) + res // d4, blk, stride=d4), :] = val

    def pair(lo, hi):
        return jnp.where(head0, lo, hi)

    def scores(idx):
        slot = idx % 2
        for br in range(n_br):
            q = load_q(br, idx)
            zero = jnp.zeros_like(q)
            q2 = jnp.concatenate([jnp.where(head0, q, zero), jnp.where(head0, zero, q)], axis=0)
            s = _dot_nt(q2, kbufs[br][key_rows(br, idx), :])
            first = (tile == 0).astype(jnp.int32) if block_in_residue(br, idx) == 0 else 0
            bias = bias_ref[first]
            maxima = []
            for h in range(2):
                r = slice(h * blk, (h + 1) * blk)
                sh = s[r] + bias
                s_ref[slot, br, r, :] = sh
                m = jnp.broadcast_to(jnp.max(sh, axis=-1, keepdims=True), (blk, LANES))
                m_ref[slot, br, h] = m
                maxima.append(m)
            store_result(br, idx, KIND_M, pair(*maxima))

    def values(idx):
        slot = idx % 2
        for br in range(n_br):
            parts = []
            for h in range(2):
                m = m_ref[slot, br, h]
                s = s_ref[slot, br, h * blk:(h + 1) * blk, :]
                parts.append(jnp.exp(s - jnp.concatenate([m, m], axis=1)).astype(BF16))
            pv = _dot(jnp.concatenate(parts, axis=0), vbufs[br][key_rows(br, idx), :])
            store_result(br, idx, KIND_ACC, pair(pv[0:blk, 0:LANES], pv[blk:2 * blk, 0:LANES]))
            store_result(br, idx, KIND_L, pair(pv[0:blk, LANES:2 * LANES], pv[blk:2 * blk, LANES:2 * LANES]))

    sub = rows // d4

    def regroup_old(c):
        for kind in range(N_KINDS):
            for a in range(d4):
                res_old[2 * N_KINDS + kind, pl.ds(a + d4 * blk * c, blk, stride=d4), :] = (
                    y_old[kind, a * sub + c * blk:a * sub + (c + 1) * blk, :])

    def combine_old(i):
        r = slice(i * blk, (i + 1) * blk)
        ms = [res_old[br * N_KINDS + KIND_M, r, :] for br in range(n_br)]
        mx = jnp.maximum(jnp.maximum(ms[0], ms[1]), ms[2])
        ws = [jnp.exp(m - mx) for m in ms]
        num = sum(w * res_old[br * N_KINDS + KIND_ACC, r, :] for br, w in enumerate(ws))
        den = sum(w * res_old[br * N_KINDS + KIND_L, r, :] for br, w in enumerate(ws))
        o_ref[0, r, :] = (num / den).astype(o_ref.dtype)

    if lead:
        scores(0)
        scores(1)
        for i in range(ATT_BLOCKS):
            values(i)
            if i + 2 < ATT_BLOCKS:
                scores(i + 2)
    for i in range(ATT_BLOCKS):
        if i % d4 == 0:
            regroup_old(i // d4)
        combine_old(i)


def _dilated(a1, a4, a16, bias):
    b, s, _ = a1.shape
    rows = ATT_ROWS
    blk = BAND_BLOCK
    pairs = ATT_WIDTH // LANES
    d4, d16 = DILATIONS[1], DILATIONS[2]
    tiles = s // rows
    n_steps = b * pairs * tiles
    decode = lambda i: (i // (pairs * tiles), (i // tiles) % pairs, i % tiles)
    of_compute = lambda f: (lambda i: f(*decode(jnp.minimum(i, n_steps - 1))))
    of_combine = lambda f: (lambda i: f(*decode(jnp.maximum(i - 1, 0))))
    prev_of = lambda n: (lambda i: jnp.maximum(i * n - 1, 0))

    def specs(dil):
        sub = rows // dil
        per = sub // blk
        if dil == 1:
            cur = lambda off: pl.BlockSpec((1, sub, LANES), of_compute(lambda bi, pi, ti: (bi, ti, off + pi)))
            prv = lambda off: pl.BlockSpec(
                (1, blk, LANES), of_compute(lambda bi, pi, ti: (bi, prev_of(per)(ti), off + pi)))
        else:
            cur = lambda off: pl.BlockSpec(
                (1, dil, sub, LANES), of_compute(lambda bi, pi, ti: (bi, 0, ti, off + pi)))
            prv = lambda off: pl.BlockSpec(
                (1, dil, blk, LANES), of_compute(lambda bi, pi, ti: (bi, 0, prev_of(per)(ti), off + pi)))
        return [cur(0), cur(pairs), prv(pairs), cur(2 * pairs), prv(2 * pairs)]

    key_rows = lambda dil: (ATT_BLOCKS + dil) * blk
    return pl.pallas_call(
        functools.partial(_dilated_kernel, tiles, n_steps + 1),
        grid=(n_steps + 1,),
        in_specs=specs(1) + specs(d4) + specs(d16) + [_resident(bias.shape)],
        out_specs=pl.BlockSpec((1, rows, LANES), of_combine(lambda bi, pi, ti: (bi, ti, pi))),
        out_shape=jax.ShapeDtypeStruct((b, s, ATT_WIDTH), BF16),
        scratch_shapes=(
            [pltpu.VMEM((key_rows(dil), LANES), BF16) for dil in DILATIONS]
            + [pltpu.VMEM((key_rows(dil), 2 * LANES), BF16) for dil in DILATIONS]
            + [pltpu.VMEM((2, len(DILATIONS), 2 * blk, 2 * blk), F32),
               pltpu.VMEM((2, len(DILATIONS), 2, blk, LANES), F32),
               pltpu.VMEM((2, len(DILATIONS) * N_KINDS, rows, LANES), F32),
               pltpu.VMEM((2, N_KINDS, rows, LANES), F32)]),
        compiler_params=pltpu.CompilerParams(
            dimension_semantics=("arbitrary",), vmem_limit_bytes=VMEM_LIMIT_BYTES),
        name="dilated",
    )(a1, a1, a1, a1, a1, a4, a4, a4, a4, a4, a16, a16, a16, a16, a16, bias)


def _out_ffn_kernel(h_ref, ret_ref, gate_ref, att_ref, gain_ref, wo_hbm, g2_ref, wg_hbm, wu_hbm, wd_hbm, gf_ref,
                    y_ref, xn_ref, act_ref, mix_ref, h2_ref, wo_ref, wg_ref, wu_ref, wd_ref, stage_ref, sem,
                    *, final_norm, n_steps):
    d = RET_HEAD_DIM

    @pl.when(pl.program_id(0) == 0)
    def _():
        _load_weights_bf16([(wo_hbm, wo_ref), (wg_hbm, wg_ref), (wu_hbm, wu_ref), (wd_hbm, wd_ref)],
                           stage_ref, sem)
        h2_ref[1] = jnp.zeros(h2_ref.shape[1:], h2_ref.dtype)

    def body(slot, lead):
        if lead:
            h2_ref[slot] = h_ref[...] + _dot(att_ref[...], wo_ref[RET_WIDTH:, :])
        h_prev = h2_ref[1 - slot]
        xn_ref[...] = _rms_norm(h_prev, g2_ref[...]).astype(BF16)
        if lead:
            for hd in range(RET_HEADS):
                lanes = slice(hd * d, (hd + 1) * d)
                ret = ret_ref[:, lanes]
                cen = ret - jnp.mean(ret, axis=-1, keepdims=True)
                var = jnp.mean(cen * cen, axis=-1, keepdims=True)
                y = cen * lax.rsqrt(var + GN_EPS) * gain_ref[:, lanes]
                gate = gate_ref[:, lanes]
                mix_ref[:, lanes] = (y * (gate * jax.nn.sigmoid(gate))).astype(BF16)
        out = h2_ref[1 - slot] + 0.5 * _swiglu_half_step(xn_ref, wg_ref, wu_ref, wd_ref, act_ref)
        if lead:
            h2_ref[slot] = h2_ref[slot] + _dot(mix_ref[...], wo_ref[0:RET_WIDTH, :])
        if final_norm:
            out = _rms_norm(out, gf_ref[...])
        y_ref[...] = out

    _run_lagged(body, n_steps)


def _out_ffn(h1, ret, gate, att, gain, wo, g2, wg, wu, wd, gf, final_norm):
    t, d = h1.shape
    d_ff = wg.shape[1]
    rows = OUT_FFN_ROWS
    n_tiles = t // rows
    by_rows = lambda a: pl.BlockSpec((rows, a.shape[1]), lambda i: (jnp.minimum(i, n_tiles - 1), 0))
    in_hbm = pl.BlockSpec(memory_space=pl.ANY)
    return pl.pallas_call(
        functools.partial(_out_ffn_kernel, final_norm=final_norm, n_steps=n_tiles + 1),
        grid=(n_tiles + 1,),
        in_specs=[
            by_rows(h1), by_rows(ret), by_rows(gate), by_rows(att),
            _resident(gain.shape),
            in_hbm,
            _resident((1, d)),
            in_hbm, in_hbm, in_hbm,
            _resident((1, d)),
        ],
        out_specs=pl.BlockSpec((rows, d), lambda i: (jnp.maximum(i - 1, 0), 0)),
        out_shape=jax.ShapeDtypeStruct((t, d), F32),
        scratch_shapes=[pltpu.VMEM((rows, d), BF16), pltpu.VMEM((rows, d_ff), BF16),
                        pltpu.VMEM((rows, RET_WIDTH), BF16), pltpu.VMEM((2, rows, d), F32)]
        + _weight_scratch(wo, wg, wu, wd),
        compiler_params=pltpu.CompilerParams(
            dimension_semantics=("arbitrary",), vmem_limit_bytes=VMEM_LIMIT_BYTES),
        name="out_ffn",
    )(h1, ret, gate, att, gain, wo, g2, wg, wu, wd, gf)


def kernel(x, norm_ffn1, ffn1_w_gate, ffn1_w_up, ffn1_w_down, norm_mix, w_in, ret_norm_gain,
           w_out, norm_ffn2, ffn2_w_gate, ffn2_w_up, ffn2_w_down, norm_final):
    b, s, d = x.shape
    depth = norm_ffn1.shape[0]
    assert s % ATT_ROWS == 0 and s % RET_ROWS == 0 and (b * s) % FFN_ROWS == 0 and (b * s) % OUT_FFN_ROWS == 0
    assert w_in.shape[2] == 4 * RET_WIDTH + 3 * ATT_WIDTH
    assert FFN_ROWS % RET_CHUNK == 0 and RET_ROWS % RET_CHUNK == 0
    rotation, zeta_rows, decay_in, xi_lanes, gamma = _retention_tables(s, FFN_ROWS)
    bias = _attention_bias()
    row = lambda v: v.reshape(1, -1)
    seq = lambda a: a.reshape(b, s, -1)

    h = x.reshape(b * s, d)
    for l in range(depth):
        h1, r, rg, a1, a4, a16 = _ffn_in(
            h, row(norm_ffn1[l]), ffn1_w_gate[l], ffn1_w_up[l], ffn1_w_down[l],
            row(norm_mix[l]), w_in[l], rotation, zeta_rows, b)
        ret = _retention(seq(r), decay_in, xi_lanes, gamma)
        att = _dilated(seq(a1), a4, a16, bias)
        h = _out_ffn(h1, ret.reshape(b * s, -1), rg, att.reshape(b * s, -1), row(ret_norm_gain[l]),
                     w_out[l], row(norm_ffn2[l]), ffn2_w_gate[l], ffn2_w_up[l], ffn2_w_down[l],
                     row(norm_final), final_norm=(l == depth - 1))
    return h.reshape(b, s, d)
```
